```python
import math, functools
import jax, jax.numpy as jnp
from jax import lax
import numpy as np

D_MODEL = 2048
BATCH = 4
SEQ = 4096
DEPTH = 1
DEC_BATCH = 32
DEC_SEQ = 8
PAST_LEN = 16384
PAGE_SIZE = 128

N_META = 16
ATT_HEAD_DIM = 64
ATT_V_DIM = 2 * ATT_HEAD_DIM
ATT_HEADS = D_MODEL // (2 * ATT_V_DIM)
ATT_WIDTH = ATT_HEADS * ATT_V_DIM
QK_DIM = ATT_HEADS * 2 * ATT_HEAD_DIM
ATT_SCALE = ATT_HEAD_DIM ** -0.5
Q_BLOCK = 128
SSM_HEAD_DIM = 64
SSM_HEADS = D_MODEL // (2 * SSM_HEAD_DIM)
SSM_WIDTH = SSM_HEADS * SSM_HEAD_DIM
SSM_GROUPS = 2
SSM_STATE = 128
CONV_WIDTH = 4
CONV_DIM = SSM_WIDTH + 2 * SSM_GROUPS * SSM_STATE
SSM_CHUNK = 128
MIX_WIDTH = ATT_WIDTH + SSM_WIDTH
SPLIT_POINTS = (QK_DIM, 2 * QK_DIM, 2 * QK_DIM + ATT_WIDTH, 2 * QK_DIM + ATT_WIDTH + SSM_WIDTH, 2 * QK_DIM + ATT_WIDTH + SSM_WIDTH + CONV_DIM)
IN_DIM = SPLIT_POINTS[-1] + SSM_HEADS
N_EXPERTS = 32
TOP_K = 4
D_FF = D_MODEL
SWIGLU_ALPHA = 1.702
SWIGLU_LIMIT = 7.0
MOE_BLOCK = 256
MOE_MIN_BLOCK = 8
EPS = 1e-5

kernel_name = 'hymba_diffattn_ssd_moe_step'


def _rmsnorm(x, w):
    xf = x.astype(jnp.float32)
    y = xf * lax.rsqrt(jnp.mean(xf * xf, axis=-1, keepdims=True) + EPS)
    return (y * w.astype(jnp.float32)).astype(x.dtype)


def _lambda_init(layer):
    return 0.8 - 0.6 * math.exp(-0.3 * layer)


def _diff_lambda(lq1, lk1, lq2, lk2, lam_init):
    f32 = jnp.float32
    return (jnp.exp(jnp.sum(lq1.astype(f32) * lk1.astype(f32)))
            - jnp.exp(jnp.sum(lq2.astype(f32) * lk2.astype(f32))) + lam_init)


def _split_projection(h, w_in_l):
    b, L, _ = h.shape
    q, k, v, z, xbc, dt = jnp.split(h @ w_in_l, SPLIT_POINTS, axis=-1)
    q = q.reshape(b, L, ATT_HEADS, 2, ATT_HEAD_DIM) * ATT_SCALE
    k = k.reshape(b, L, ATT_HEADS, 2, ATT_HEAD_DIM)
    v = v.reshape(b, L, ATT_HEADS, ATT_V_DIM)
    return q, k, v, z, xbc, dt


def _diff_probs(q, k, mask):
    s = jnp.einsum('bqhcd,bkhcd->bhcqk', q, k).astype(jnp.float32)
    return jax.nn.softmax(jnp.where(mask, s, -jnp.inf), axis=-1)


def _diff_combine(p, v, lam):
    a = p[:, :, 0] - lam * p[:, :, 1]
    return jnp.einsum('bhqk,bkhe->bqhe', a, v.astype(jnp.float32))


def _diff_attn_prompt(q, k, v, lam):
    b, L = q.shape[:2]
    pos = jnp.arange(L)
    meta_mask = pos[:N_META, None] >= pos[None, :N_META]
    o_meta = _diff_combine(_diff_probs(q[:, :N_META], k[:, :N_META], meta_mask), v[:, :N_META], lam)
    n_blk = (L - N_META) // Q_BLOCK
    qb = q[:, N_META:].reshape(b, n_blk, Q_BLOCK, ATT_HEADS, 2, ATT_HEAD_DIM).transpose(1, 0, 2, 3, 4, 5)

    def block(args):
        q_blk, start = args
        qpos = N_META + start + jnp.arange(Q_BLOCK)
        mask = qpos[:, None] >= pos[None, :]
        return _diff_combine(_diff_probs(q_blk, k, mask), v, lam)

    o_real = lax.map(block, (qb, jnp.arange(n_blk) * Q_BLOCK))
    o_real = o_real.transpose(1, 0, 2, 3, 4).reshape(b, L - N_META, ATT_HEADS, ATT_V_DIM)
    return jnp.concatenate([o_meta, o_real], axis=1)


def _diff_attn_sample(q, k, v, lam, cache_k, cache_v, layer, page_table):
    f32 = jnp.float32
    S = q.shape[1]
    causal = jnp.tril(jnp.ones((S, S), dtype=bool))
    s = jnp.einsum('bqhcd,bkhcd->bhcqk', q, k).astype(f32)
    s = jnp.where(causal, s, -jnp.inf)
    m = jnp.max(s, axis=-1)
    p = jnp.exp(s - m[..., None])
    l = jnp.sum(p, axis=-1)
    acc = jnp.einsum('bhcqk,bkhe->bhcqe', p, v.astype(f32))

    def step(carry, pages):
        m, l, acc = carry
        kp = cache_k[layer, pages]
        vp = cache_v[layer, pages]
        s = jnp.einsum('bqhcd,bkhcd->bhcqk', q, kp).astype(f32)
        m_new = jnp.maximum(m, jnp.max(s, axis=-1))
        corr = jnp.exp(m - m_new)
        p = jnp.exp(s - m_new[..., None])
        l = l * corr + jnp.sum(p, axis=-1)
        acc = acc * corr[..., None] + jnp.einsum('bhcqk,bkhe->bhcqe', p, vp.astype(f32))
        return (m_new, l, acc), None

    (m, l, acc), _ = lax.scan(step, (m, l, acc), page_table.T)
    o = acc / l[..., None]
    o = o[:, :, 0] - lam * o[:, :, 1]
    return o.transpose(0, 2, 1, 3)


def _diff_head_out(o, subln_w_l, lam_init):
    b, L = o.shape[:2]
    return (_rmsnorm(o, subln_w_l) * (1.0 - lam_init)).reshape(b, L, ATT_WIDTH)


def _causal_dwconv(xbc, conv_state, w, bias):
    full = jnp.concatenate([conv_state.astype(xbc.dtype), xbc], axis=1)
    out = lax.conv_general_dilated(full, w[:, None, :].astype(xbc.dtype), window_strides=(1,), padding='VALID',
                                   dimension_numbers=('NWC', 'WIO', 'NWC'), feature_group_count=CONV_DIM)
    return jax.nn.silu(out + bias.astype(xbc.dtype)), full[:, -(CONV_WIDTH - 1):]


def _ssd_chunked(x, dt, a, bm, cm, init_state, chunk):
    f32 = jnp.float32
    b, L = x.shape[:2]
    pad = (-L) % chunk
    x = jnp.pad(x.astype(f32), ((0, 0), (0, pad), (0, 0), (0, 0)))
    dt = jnp.pad(dt, ((0, 0), (0, pad), (0, 0)))
    bm = jnp.pad(bm.astype(f32), ((0, 0), (0, pad), (0, 0), (0, 0)))
    cm = jnp.pad(cm.astype(f32), ((0, 0), (0, pad), (0, 0), (0, 0)))
    n_c = (L + pad) // chunk
    hg = SSM_HEADS // SSM_GROUPS
    xr = x.reshape(b, n_c, chunk, SSM_GROUPS, hg, SSM_HEAD_DIM)
    dtr = dt.reshape(b, n_c, chunk, SSM_GROUPS, hg)
    br = bm.reshape(b, n_c, chunk, SSM_GROUPS, SSM_STATE)
    cr = cm.reshape(b, n_c, chunk, SSM_GROUPS, SSM_STATE)
    xdt = xr * dtr[..., None]
    a_cum = jnp.cumsum((dtr * a.reshape(SSM_GROUPS, hg)).transpose(0, 3, 4, 1, 2), axis=-1)
    causal = jnp.tril(jnp.ones((chunk, chunk), dtype=bool))
    seg = jnp.exp(jnp.where(causal, a_cum[..., :, None] - a_cum[..., None, :], -jnp.inf))
    y_diag = jnp.einsum('bclgn,bcsgn,bghcls,bcsghp->bclghp', cr, br, seg, xdt)
    decay_to_end = jnp.exp(a_cum[..., -1:] - a_cum)
    chunk_states = jnp.einsum('bcsgn,bghcs,bcsghp->bcghpn', br, decay_to_end, xdt)
    chunk_decay = jnp.exp(a_cum[..., -1])

    def step(state, inp):
        st, dec = inp
        return state * dec[..., None, None] + st, state

    h0 = init_state.astype(f32).reshape(b, SSM_GROUPS, hg, SSM_HEAD_DIM, SSM_STATE)
    final, prev = lax.scan(step, h0, (chunk_states.transpose(1, 0, 2, 3, 4, 5), chunk_decay.transpose(3, 0, 1, 2)))
    prev = prev.transpose(1, 0, 2, 3, 4, 5)
    y_off = jnp.einsum('bclgn,bghcl,bcghpn->bclghp', cr, jnp.exp(a_cum), prev)
    y = (y_diag + y_off).reshape(b, n_c * chunk, SSM_HEADS, SSM_HEAD_DIM)[:, :L]
    return y, final.reshape(b, SSM_HEADS, SSM_HEAD_DIM, SSM_STATE)


def _ssd_mixer(z, xbc, dt_raw, conv_state, ssm_state, conv_w_l, conv_b_l, dt_bias_l, a_log_l, d_skip_l, norm_w_l, meta_split):
    f32 = jnp.float32
    b, L, _ = xbc.shape
    xbc_c, new_conv = _causal_dwconv(xbc, conv_state, conv_w_l, conv_b_l)
    xs, bm, cm = jnp.split(xbc_c, (SSM_WIDTH, SSM_WIDTH + SSM_GROUPS * SSM_STATE), axis=-1)
    xs = xs.reshape(b, L, SSM_HEADS, SSM_HEAD_DIM)
    bm = bm.reshape(b, L, SSM_GROUPS, SSM_STATE)
    cm = cm.reshape(b, L, SSM_GROUPS, SSM_STATE)
    dt = jax.nn.softplus(dt_raw.astype(f32) + dt_bias_l.astype(f32))
    a = -jnp.exp(a_log_l.astype(f32))
    if meta_split:
        y0, s0 = _ssd_chunked(xs[:, :N_META], dt[:, :N_META], a, bm[:, :N_META], cm[:, :N_META], ssm_state, N_META)
        y1, new_state = _ssd_chunked(xs[:, N_META:], dt[:, N_META:], a, bm[:, N_META:], cm[:, N_META:], s0, SSM_CHUNK)
        y = jnp.concatenate([y0, y1], axis=1)
    else:
        y, new_state = _ssd_chunked(xs, dt, a, bm, cm, ssm_state, min(SSM_CHUNK, L))
    y = y + d_skip_l.astype(f32)[:, None] * xs.astype(f32)
    y = y.reshape(b, L, SSM_WIDTH) * jax.nn.silu(z.astype(f32))
    y = _rmsnorm(y.reshape(b, L, SSM_GROUPS, SSM_WIDTH // SSM_GROUPS), norm_w_l.reshape(SSM_GROUPS, SSM_WIDTH // SSM_GROUPS))
    return y.reshape(b, L, SSM_WIDTH).astype(z.dtype), new_conv, new_state


def _clamped_swiglu(h):
    gate = jnp.minimum(h[..., :D_FF], SWIGLU_LIMIT)
    up = jnp.clip(h[..., D_FF:], -SWIGLU_LIMIT, SWIGLU_LIMIT)
    return gate * jax.nn.sigmoid(SWIGLU_ALPHA * gate) * (up + 1.0)


def _moe_block_rows(m):
    blk = MOE_BLOCK
    while blk > MOE_MIN_BLOCK and blk * N_EXPERTS > m:
        blk //= 2
    return blk


def _moe_ffn(x, layer, w_router, b_router, w_gate_up, b_gate_up, w_down, b_down):
    T, D = x.shape
    logits = (x @ w_router[layer]).astype(jnp.float32) + b_router[layer].astype(jnp.float32)
    top_v, top_e = lax.top_k(logits, TOP_K)
    gates = jax.nn.softmax(top_v, axis=-1)
    M = T * TOP_K
    blk = _moe_block_rows(M)
    n_blocks = -(-M // blk) + N_EXPERTS
    flat_e = top_e.reshape(-1)
    order = jnp.argsort(flat_e)
    sorted_e = flat_e[order]
    counts = jnp.bincount(flat_e, length=N_EXPERTS)
    padded = ((counts + blk - 1) // blk) * blk
    start = jnp.cumsum(counts) - counts
    pend = jnp.cumsum(padded)
    pstart = pend - padded
    dest = (pstart[sorted_e] + jnp.arange(M) - start[sorted_e]).astype(jnp.int32)
    slot_tok = jnp.full((n_blocks * blk,), T, jnp.int32).at[dest].set((order // TOP_K).astype(jnp.int32))
    block_e = jnp.minimum(jnp.searchsorted(pend, jnp.arange(n_blocks) * blk, side='right'), N_EXPERTS - 1)
    x_pad = jnp.concatenate([x, jnp.zeros((1, D), x.dtype)], axis=0)
    xb = x_pad[slot_tok].reshape(n_blocks, blk, D)

    def expert_block(args):
        xe, e = args
        h = (xe @ w_gate_up[layer, e] + b_gate_up[layer, e]).astype(jnp.float32)
        return _clamped_swiglu(h).astype(xe.dtype) @ w_down[layer, e] + b_down[layer, e]

    yb = lax.map(expert_block, (xb, block_e)).reshape(n_blocks * blk, D)
    slot_of = jnp.zeros((M,), jnp.int32).at[order].set(dest)
    y = yb[slot_of].reshape(T, TOP_K, D)
    return jnp.einsum('tk,tkd->td', gates.astype(y.dtype), y)


def setup_inputs(seed: int = 0) -> dict:
    key = jax.random.key(seed)
    ks = jax.random.split(key, 32)
    f32 = jnp.float32
    n_pages = PAST_LEN // PAGE_SIZE
    n_used = DEC_BATCH * n_pages
    n_pool = n_used + max(1, n_used // 4)
    nrm = lambda k, shape, s: jax.random.normal(k, shape, f32) * s
    dt0 = jnp.exp(jax.random.uniform(ks[10], (DEPTH, SSM_HEADS), f32) * (math.log(0.1) - math.log(0.001)) + math.log(0.001))
    return {
        'x_prompt': nrm(ks[0], (BATCH, SEQ, D_MODEL), 1.0),
        'x_sample': nrm(ks[1], (DEC_BATCH, DEC_SEQ, D_MODEL), 1.0),
        'cache_k': nrm(ks[2], (DEPTH, n_pool, PAGE_SIZE, ATT_HEADS, 2, ATT_HEAD_DIM), 1.0),
        'cache_v': nrm(ks[3], (DEPTH, n_pool, PAGE_SIZE, ATT_HEADS, ATT_V_DIM), 1.0),
        'page_table': jax.random.permutation(ks[4], n_pool)[:n_used].reshape(DEC_BATCH, n_pages).astype(jnp.int32),
        'state_ssm': nrm(ks[5], (DEPTH, DEC_BATCH, SSM_HEADS, SSM_HEAD_DIM, SSM_STATE), 0.1),
        'state_conv': nrm(ks[6], (DEPTH, DEC_BATCH, CONV_WIDTH - 1, CONV_DIM), 1.0),
        'meta_tokens': nrm(ks[7], (N_META, D_MODEL), 1.0),
        'norm_mix_w': 1.0 + nrm(ks[8], (DEPTH, D_MODEL), 0.01),
        'w_in': nrm(ks[9], (DEPTH, D_MODEL, IN_DIM), D_MODEL ** -0.5),
        'lambda_q1': nrm(ks[11], (DEPTH, ATT_HEAD_DIM), 0.1),
        'lambda_k1': nrm(ks[12], (DEPTH, ATT_HEAD_DIM), 0.1),
        'lambda_q2': nrm(ks[13], (DEPTH, ATT_HEAD_DIM), 0.1),
        'lambda_k2': nrm(ks[14], (DEPTH, ATT_HEAD_DIM), 0.1),
        'subln_w': 1.0 + nrm(ks[15], (DEPTH, ATT_V_DIM), 0.01),
        'conv_w': nrm(ks[16], (DEPTH, CONV_WIDTH, CONV_DIM), CONV_WIDTH ** -0.5),
        'conv_b': nrm(ks[17], (DEPTH, CONV_DIM), 0.01),
        'dt_bias': dt0 + jnp.log(-jnp.expm1(-dt0)),
        'a_log': jnp.log(jax.random.uniform(ks[18], (DEPTH, SSM_HEADS), f32, 1.0, 16.0)),
        'd_skip': 1.0 + nrm(ks[19], (DEPTH, SSM_HEADS), 0.01),
        'ssm_norm_w': 1.0 + nrm(ks[20], (DEPTH, SSM_WIDTH), 0.01),
        'w_out': nrm(ks[21], (DEPTH, MIX_WIDTH, D_MODEL), MIX_WIDTH ** -0.5),
        'norm_ffn_w': 1.0 + nrm(ks[22], (DEPTH, D_MODEL), 0.01),
        'w_router': nrm(ks[23], (DEPTH, D_MODEL, N_EXPERTS), D_MODEL ** -0.5),
        'b_router': nrm(ks[24], (DEPTH, N_EXPERTS), 0.01),
        'w_gate_up': nrm(ks[25], (DEPTH, N_EXPERTS, D_MODEL, 2 * D_FF), D_MODEL ** -0.5),
        'b_gate_up': nrm(ks[26], (DEPTH, N_EXPERTS, 2 * D_FF), 0.01),
        'w_down': nrm(ks[27], (DEPTH, N_EXPERTS, D_FF, D_MODEL), D_FF ** -0.5),
        'b_down': nrm(ks[28], (DEPTH, N_EXPERTS, D_MODEL), 0.01),
        'norm_final_w': 1.0 + nrm(ks[29], (D_MODEL,), 0.01),
    }


def reference(x_prompt, x_sample, cache_k, cache_v, page_table, state_ssm, state_conv, meta_tokens,
              norm_mix_w, w_in, lambda_q1, lambda_k1, lambda_q2, lambda_k2, subln_w, conv_w, conv_b,
              dt_bias, a_log, d_skip, ssm_norm_w, w_out, norm_ffn_w, w_router, b_router,
              w_gate_up, b_gate_up, w_down, b_down, norm_final_w):
    f32 = jnp.float32
    dtype = x_prompt.dtype
    b_p = x_prompt.shape[0]
    meta = jnp.broadcast_to(meta_tokens.astype(dtype)[None], (b_p, N_META, D_MODEL))
    xp = jnp.concatenate([meta, x_prompt], axis=1)
    xs = x_sample

    def run_layer(x, layer, attend, conv_state, ssm_state, meta_split):
        lam_init = _lambda_init(layer)
        lam = _diff_lambda(lambda_q1[layer], lambda_k1[layer], lambda_q2[layer], lambda_k2[layer], lam_init)
        h = _rmsnorm(x, norm_mix_w[layer])
        q, k, v, z, xbc, dt = _split_projection(h, w_in[layer])
        o_att = _diff_head_out(attend(q, k, v, lam), subln_w[layer], lam_init).astype(x.dtype)
        o_ssm, new_conv, new_ssm = _ssd_mixer(z, xbc, dt, conv_state, ssm_state, conv_w[layer], conv_b[layer],
                                              dt_bias[layer], a_log[layer], d_skip[layer], ssm_norm_w[layer], meta_split)
        x = x + jnp.concatenate([o_att, o_ssm], axis=-1) @ w_out[layer]
        h2 = _rmsnorm(x, norm_ffn_w[layer])
        y_ffn = _moe_ffn(h2.reshape(-1, D_MODEL), layer, w_router, b_router, w_gate_up, b_gate_up, w_down, b_down)
        x = x + y_ffn.reshape(x.shape).astype(x.dtype)
        return x, k, v, new_conv, new_ssm

    k_p_l, v_p_l, conv_p_l, ssm_p_l = [], [], [], []
    k_s_l, v_s_l, conv_s_l, ssm_s_l = [], [], [], []
    for layer in range(DEPTH):
        zero_conv = jnp.zeros((b_p, CONV_WIDTH - 1, CONV_DIM), dtype)
        zero_ssm = jnp.zeros((b_p, SSM_HEADS, SSM_HEAD_DIM, SSM_STATE), f32)
        xp, k_p, v_p, conv_p, ssm_p = run_layer(xp, layer, _diff_attn_prompt, zero_conv, zero_ssm, True)
        attend_s = functools.partial(_diff_attn_sample, cache_k=cache_k, cache_v=cache_v, layer=layer, page_table=page_table)
        xs, k_s, v_s, conv_s, ssm_s = run_layer(xs, layer, attend_s, state_conv[layer], state_ssm[layer], False)
        k_p_l.append(k_p); v_p_l.append(v_p); conv_p_l.append(conv_p); ssm_p_l.append(ssm_p)
        k_s_l.append(k_s); v_s_l.append(v_s); conv_s_l.append(conv_s); ssm_s_l.append(ssm_s)

    y_prompt = _rmsnorm(xp[:, N_META:], norm_final_w)
    y_sample = _rmsnorm(xs, norm_final_w)
    return (y_prompt, y_sample, jnp.stack(k_p_l), jnp.stack(v_p_l), jnp.stack(k_s_l), jnp.stack(v_s_l),
            jnp.stack(ssm_p_l), jnp.stack(conv_p_l), jnp.stack(ssm_s_l), jnp.stack(conv_s_l))
```

```python
import functools
import math

import jax
import jax.numpy as jnp
from jax import lax
from jax.experimental import pallas as pl
from jax.experimental.pallas import tpu as pltpu

F32 = jnp.float32
BF16 = jnp.bfloat16

N_META = 16
ATT_HEAD_DIM = 64
ATT_V_DIM = 128
SSM_HEAD_DIM = 64
SSM_GROUPS = 2
SSM_STATE = 128
CONV_WIDTH = 4
TOP_K = 4
SWIGLU_ALPHA = 1.702
SWIGLU_LIMIT = 7.0
EPS = 1e-5
ATT_SCALE = ATT_HEAD_DIM ** -0.5

LANES = 128
SUBLANES = 8
VMEM_LIMIT_BYTES = 56 * 1024 * 1024
NEG_INF = float("-inf")
CONV_PAD = SUBLANES


def _tile(n, pref):
    if n <= pref:
        return n
    for t in range(pref, 0, -1):
        if n % t == 0 and t % SUBLANES == 0:
            return t
    return n


def _cparams(sem):
    return pltpu.CompilerParams(dimension_semantics=sem, vmem_limit_bytes=VMEM_LIMIT_BYTES)


def _lambda_init(layer):
    return 0.8 - 0.6 * math.exp(-0.3 * layer)


def _lam_from_ref(lam_ref, lam_init):
    lv = lam_ref[...]
    s1 = jnp.sum(lv[0:1] * lv[1:2], axis=-1, keepdims=True)
    s2 = jnp.sum(lv[2:3] * lv[3:4], axis=-1, keepdims=True)
    return jnp.exp(s1) - jnp.exp(s2) + lam_init


def _split3(x):
    hi = x.astype(BF16)
    r1 = x - hi.astype(F32)
    mid = r1.astype(BF16)
    lo = (r1 - mid.astype(F32)).astype(BF16)
    return hi, mid, lo


def _dot(a, b):
    return jnp.dot(a, b, preferred_element_type=F32)


def _dot_nt(a, b):
    return lax.dot_general(a, b, (((1,), (1,)), ((), ())), preferred_element_type=F32)


def _rms_proj_kernel(x_ref, nw_ref, w_ref, wdt_ref, proj_ref, dt_ref, h_ref):
    @pl.when(pl.program_id(1) == 0)
    def _():
        x = x_ref[...]
        ms = jnp.mean(x * x, axis=-1, keepdims=True)
        h = ((x * lax.rsqrt(ms + EPS)) * nw_ref[...]).astype(BF16)
        h_ref[...] = h
        dt_ref[...] = _dot(h, wdt_ref[...])

    proj_ref[...] = _dot(h_ref[...], w_ref[...])


def _rms_proj(x2d, norm_w, w_main, w_dt):
    T, D = x2d.shape
    N = w_main.shape[1]
    bm = _tile(T, 1024)
    bn = _tile(N, 512)
    return pl.pallas_call(
        _rms_proj_kernel,
        grid=(T // bm, N // bn),
        in_specs=[
            pl.BlockSpec((bm, D), lambda i, j: (i, 0)),
            pl.BlockSpec((1, D), lambda i, j: (0, 0)),
            pl.BlockSpec((D, bn), lambda i, j: (0, j)),
            pl.BlockSpec((D, LANES), lambda i, j: (0, 0)),
        ],
        out_specs=[
            pl.BlockSpec((bm, bn), lambda i, j: (i, j)),
            pl.BlockSpec((bm, LANES), lambda i, j: (i, 0)),
        ],
        out_shape=[jax.ShapeDtypeStruct((T, N), F32), jax.ShapeDtypeStruct((T, LANES), F32)],
        scratch_shapes=[pltpu.VMEM((bm, D), BF16)],
        compiler_params=_cparams(("parallel", "arbitrary")),
        name="rms_proj",
    )(x2d, norm_w.reshape(1, D), w_main, w_dt)


def _flash_kernel(*refs, tq, has_prefix, lam_init):
    if has_prefix:
        (qi_tab, ki_tab, lam_ref, sub_ref, q_ref, k_ref, v_ref, pk_ref, pv_ref,
         o_ref, m_ref, l_ref, acc_ref) = refs
    else:
        (qi_tab, ki_tab, lam_ref, sub_ref, q_ref, k_ref, v_ref,
         o_ref, m_ref, l_ref, acc_ref) = refs
    t = pl.program_id(2)
    qi = qi_tab[t]
    ki = ki_tab[t]

    q = q_ref[0] * ATT_SCALE
    lane = lax.broadcasted_iota(jnp.int32, q.shape, 1)
    q_maps = (jnp.where(lane < ATT_HEAD_DIM, q, 0.0).astype(BF16),
              jnp.where(lane >= ATT_HEAD_DIM, q, 0.0).astype(BF16))

    def update(k, v, mask):
        kb = k.astype(BF16)
        vb = v.astype(BF16)
        for c in range(2):
            s = _dot_nt(q_maps[c], kb)
            if mask is not None:
                s = jnp.where(mask, s, NEG_INF)
            m_prev = m_ref[c]
            m_new = jnp.maximum(m_prev, jnp.max(s, axis=-1, keepdims=True))
            corr = jnp.exp(m_prev - m_new)
            p = jnp.exp(s - m_new)
            l_ref[c] = l_ref[c] * corr + jnp.sum(p, axis=-1, keepdims=True)
            acc_ref[c] = acc_ref[c] * corr + _dot(p.astype(BF16), vb)
            m_ref[c] = m_new

    @pl.when(ki == 0)
    def _():
        m_ref[...] = jnp.full(m_ref.shape, NEG_INF, F32)
        l_ref[...] = jnp.zeros(l_ref.shape, F32)
        acc_ref[...] = jnp.zeros(acc_ref.shape, F32)
        if has_prefix:
            update(pk_ref[...], pv_ref[...], None)

    @pl.when(ki < qi)
    def _():
        update(k_ref[0], v_ref[0], None)

    @pl.when(ki == qi)
    def _():
        row = lax.broadcasted_iota(jnp.int32, (tq, tq), 0)
        col = lax.broadcasted_iota(jnp.int32, (tq, tq), 1)
        update(k_ref[0], v_ref[0], row >= col)
        lam = _lam_from_ref(lam_ref, lam_init)
        o = acc_ref[0] / l_ref[0] - lam * (acc_ref[1] / l_ref[1])
        ms = jnp.mean(o * o, axis=-1, keepdims=True)
        y = (o * lax.rsqrt(ms + EPS)) * sub_ref[...]
        o_ref[0] = (y * (1.0 - lam_init)).astype(o_ref.dtype)


def _flash_attn(proj3, prefix, lam_vec, subln_w, lam_init, n_heads, out_dtype=BF16):
    B, L, _ = proj3.shape
    tq = _tile(L, 512)
    nq = L // tq
    pairs = [(a, b) for a in range(nq) for b in range(a + 1)]
    qi_tab = jnp.asarray([p[0] for p in pairs], jnp.int32)
    ki_tab = jnp.asarray([p[1] for p in pairs], jnp.int32)
    H = n_heads
    has_prefix = prefix is not None
    in_specs = [
        pl.BlockSpec((4, ATT_HEAD_DIM), lambda b, h, t, qt, kt: (0, 0)),
        pl.BlockSpec((1, ATT_V_DIM), lambda b, h, t, qt, kt: (0, 0)),
        pl.BlockSpec((1, tq, LANES), lambda b, h, t, qt, kt: (b, qt[t], h)),
        pl.BlockSpec((1, tq, LANES), lambda b, h, t, qt, kt: (b, kt[t], H + h)),
        pl.BlockSpec((1, tq, LANES), lambda b, h, t, qt, kt: (b, kt[t], 2 * H + h)),
    ]
    args = [lam_vec, subln_w.reshape(1, ATT_V_DIM), proj3, proj3, proj3]
    if has_prefix:
        P = prefix.shape[0]
        in_specs += [
            pl.BlockSpec((P, LANES), lambda b, h, t, qt, kt: (0, H + h)),
            pl.BlockSpec((P, LANES), lambda b, h, t, qt, kt: (0, 2 * H + h)),
        ]
        args += [prefix, prefix]
    grid_spec = pltpu.PrefetchScalarGridSpec(
        num_scalar_prefetch=2,
        grid=(B, H, len(pairs)),
        in_specs=in_specs,
        out_specs=pl.BlockSpec((1, tq, LANES), lambda b, h, t, qt, kt: (b, qt[t], h)),
        scratch_shapes=[
            pltpu.VMEM((2, tq, 1), F32),
            pltpu.VMEM((2, tq, 1), F32),
            pltpu.VMEM((2, tq, ATT_V_DIM), F32),
        ],
    )
    return pl.pallas_call(
        functools.partial(_flash_kernel, tq=tq, has_prefix=has_prefix, lam_init=lam_init),
        grid_spec=grid_spec,
        out_shape=jax.ShapeDtypeStruct((B, L, H * ATT_V_DIM), out_dtype),
        compiler_params=_cparams(("parallel", "parallel", "arbitrary")),
        name="flash_diff_attn",
    )(qi_tab, ki_tab, *args)


def _sample_attn_kernel(*refs, pps, n_heads, s_new, page, lam_init):
    pt_ref, lam_ref, sub_ref, q_ref, kn_ref, vn_ref = refs[:6]
    k_refs = refs[6:6 + pps]
    v_refs = refs[6 + pps:6 + 2 * pps]
    o_ref, qbd_ref, m_ref, l_ref, acc_ref = refs[6 + 2 * pps:]
    step = pl.program_id(1)
    n_rows = n_heads * 2 * s_new
    width = n_heads * 2 * ATT_HEAD_DIM

    def update(ks, vs, mask):
        qbd = qbd_ref[...]
        ss = []
        for k in ks:
            s = _dot_nt(qbd, k.astype(BF16))
            if mask is not None:
                s = jnp.where(mask, s, NEG_INF)
            ss.append(s)
        m_prev = m_ref[...]
        m_new = m_prev
        for s in ss:
            m_new = jnp.maximum(m_new, jnp.max(s, axis=-1, keepdims=True))
        corr = jnp.exp(m_prev - m_new)
        ps = [jnp.exp(s - m_new) for s in ss]
        l_new = l_ref[...] * corr
        for p in ps:
            l_new = l_new + jnp.sum(p, axis=-1, keepdims=True)
        l_ref[...] = l_new
        m_ref[...] = m_new
        pbs = [p.astype(BF16) for p in ps]
        vbs = [v.astype(BF16) for v in vs]
        for h in range(n_heads):
            r0, r1 = h * 2 * s_new, (h + 1) * 2 * s_new
            pv = None
            for pb, vb in zip(pbs, vbs):
                d = _dot(pb[r0:r1, :], vb[:, h * ATT_V_DIM:(h + 1) * ATT_V_DIM])
                pv = d if pv is None else pv + d
            acc_ref[r0:r1, :] = acc_ref[r0:r1, :] * corr[r0:r1] + pv

    @pl.when(step == 0)
    def _():
        q = q_ref[0] * ATT_SCALE
        qt = jnp.concatenate([q] * (2 * n_heads), axis=0)
        row_hc = lax.broadcasted_iota(jnp.int32, (n_rows, width), 0) // s_new
        col_hc = lax.broadcasted_iota(jnp.int32, (n_rows, width), 1) // ATT_HEAD_DIM
        qbd_ref[...] = jnp.where(row_hc == col_hc, qt, 0.0).astype(BF16)
        m_ref[...] = jnp.full(m_ref.shape, NEG_INF, F32)
        l_ref[...] = jnp.zeros(l_ref.shape, F32)
        acc_ref[...] = jnp.zeros(acc_ref.shape, F32)
        pad = jnp.zeros((page - s_new, width), F32)
        k_self = jnp.concatenate([kn_ref[0], pad], axis=0)
        v_self = jnp.concatenate([vn_ref[0], pad], axis=0)
        r = lax.broadcasted_iota(jnp.int32, (n_rows, page), 0)
        c = lax.broadcasted_iota(jnp.int32, (n_rows, page), 1)
        update([k_self], [v_self], c <= (r % s_new))

    update([k[0] for k in k_refs], [v[0] for v in v_refs], None)

    @pl.when(step == pl.num_programs(1) - 1)
    def _():
        lam = _lam_from_ref(lam_ref, lam_init)
        o_all = acc_ref[...] / l_ref[...]
        for h in range(n_heads):
            r0 = h * 2 * s_new
            o = o_all[r0:r0 + s_new] - lam * o_all[r0 + s_new:r0 + 2 * s_new]
            ms = jnp.mean(o * o, axis=-1, keepdims=True)
            y = (o * lax.rsqrt(ms + EPS)) * sub_ref[...]
            o_ref[0, :, h * ATT_V_DIM:(h + 1) * ATT_V_DIM] = y * (1.0 - lam_init)


def _sample_attn(proj_s3, cache_k_l, cache_v_l, page_table, lam_vec, subln_w, lam_init, n_heads):
    b, S, _ = proj_s3.shape
    n_pool, page, width = cache_k_l.shape
    n_pages = page_table.shape[1]
    pps = 4 if n_pages % 4 == 0 else 1
    n_rows = n_heads * 2 * S

    def page_spec(j):
        return pl.BlockSpec((1, page, width), lambda i, p, pt: (pt[i, p * pps + j], 0, 0))

    in_specs = [
        pl.BlockSpec((4, ATT_HEAD_DIM), lambda i, p, pt: (0, 0)),
        pl.BlockSpec((1, ATT_V_DIM), lambda i, p, pt: (0, 0)),
        pl.BlockSpec((1, S, width), lambda i, p, pt: (i, 0, 0)),
        pl.BlockSpec((1, S, width), lambda i, p, pt: (i, 0, 1)),
        pl.BlockSpec((1, S, width), lambda i, p, pt: (i, 0, 2)),
    ] + [page_spec(j) for j in range(pps)] + [page_spec(j) for j in range(pps)]
    grid_spec = pltpu.PrefetchScalarGridSpec(
        num_scalar_prefetch=1,
        grid=(b, n_pages // pps),
        in_specs=in_specs,
        out_specs=pl.BlockSpec((1, S, width), lambda i, p, pt: (i, 0, 0)),
        scratch_shapes=[
            pltpu.VMEM((n_rows, width), BF16),
            pltpu.VMEM((n_rows, 1), F32),
            pltpu.VMEM((n_rows, 1), F32),
            pltpu.VMEM((n_rows, ATT_V_DIM), F32),
        ],
    )
    return pl.pallas_call(
        functools.partial(_sample_attn_kernel, pps=pps, n_heads=n_heads, s_new=S, page=page,
                          lam_init=lam_init),
        grid_spec=grid_spec,
        out_shape=jax.ShapeDtypeStruct((b, S, width), F32),
        compiler_params=_cparams(("parallel", "arbitrary")),
        name="sample_paged_attn",
    )(page_table, lam_vec, subln_w.reshape(1, ATT_V_DIM), proj_s3, proj_s3, proj_s3,
      *([cache_k_l] * pps), *([cache_v_l] * pps))


def _ssd_kernel(z_ref, xa_ref, xb_ref, bc_ref, dt_ref, dtt_ref, conv0_ref, s0_ref,
                cw_ref, cb_ref, dtb_ref, dtbt_ref, alog_ref, alogt_ref, dsk_ref, nw_ref, ex_ref,
                y_ref, convo_ref, so_ref, buf_ref, st_ref, *, Q, valid, n_heads):
    c = pl.program_id(1)
    n_c = pl.num_programs(1)
    ssm_w = n_heads * SSM_HEAD_DIM
    gw = ssm_w // SSM_GROUPS
    hg = n_heads // SSM_GROUPS

    @pl.when(c == 0)
    def _():
        buf_ref[0:CONV_PAD, :] = conv0_ref[0]
        st_ref[...] = s0_ref[0]

    buf_ref[CONV_PAD:CONV_PAD + Q, 0:gw] = xa_ref[0]
    buf_ref[CONV_PAD:CONV_PAD + Q, gw:2 * gw] = xb_ref[0]
    buf_ref[CONV_PAD:CONV_PAD + Q, 2 * gw:3 * gw] = bc_ref[0]
    base = CONV_PAD - (CONV_WIDTH - 1)
    conv = cb_ref[...]
    for w in range(CONV_WIDTH):
        conv = conv + buf_ref[base + w:base + w + Q, :] * cw_ref[w:w + 1, :]
    xbc = conv * jax.nn.sigmoid(conv)
    xs = xbc[:, :ssm_w]
    bmat = xbc[:, ssm_w:ssm_w + SSM_GROUPS * SSM_STATE]
    cmat = xbc[:, ssm_w + SSM_GROUPS * SSM_STATE:]

    lane = lax.broadcasted_iota(jnp.int32, (1, LANES), 1)
    a_row = jnp.where(lane < n_heads, -jnp.exp(alog_ref[...]), 0.0)
    dt = jax.nn.softplus(dt_ref[0] + dtb_ref[...])
    a_col = -jnp.exp(alogt_ref[...])
    dtt = jax.nn.softplus(dtt_ref[0] + dtbt_ref[...])
    if valid < Q:
        dt = jnp.where(lax.broadcasted_iota(jnp.int32, dt.shape, 0) < valid, dt, 0.0)
        dtt = jnp.where(lax.broadcasted_iota(jnp.int32, dtt.shape, 1) < valid, dtt, 0.0)
    da = dt * a_row
    dat = dtt * a_col

    r_i = lax.broadcasted_iota(jnp.int32, (Q, Q), 0)
    c_i = lax.broadcasted_iota(jnp.int32, (Q, Q), 1)
    causal = r_i >= c_i
    tri_l = jnp.where(causal, 1.0, 0.0).astype(BF16)
    tri_u = jnp.where(r_i <= c_i, 1.0, 0.0).astype(BF16)
    da3 = _split3(da)
    a_cum = _dot(tri_l, da3[0]) + _dot(tri_l, da3[1]) + _dot(tri_l, da3[2])
    dat3 = _split3(dat)
    a_cum_t = _dot(dat3[0], tri_u) + _dot(dat3[1], tri_u) + _dot(dat3[2], tri_u)

    ex = ex_ref[...]
    dt3 = _split3(dt)
    dt_x = _dot(dt3[0], ex) + _dot(dt3[1], ex) + _dot(dt3[2], ex)
    ac3 = _split3(a_cum)
    acum_x = _dot(ac3[0], ex) + _dot(ac3[1], ex) + _dot(ac3[2], ex)
    a_last_x = acum_x[Q - 1:Q, :]
    e_in = jnp.exp(acum_x)
    dec_end = jnp.exp(a_last_x - acum_x)
    chunk_decay = jnp.exp(a_last_x)

    xdt = xs * dt_x
    xdt_b = xdt.astype(BF16)
    xdec_b = (xdt * dec_end).astype(BF16)
    lane_q = lax.broadcasted_iota(jnp.int32, (Q, LANES), 1)

    y_groups = []
    for g in range(SSM_GROUPS):
        cg = cmat[:, g * SSM_STATE:(g + 1) * SSM_STATE].astype(BF16)
        bg = bmat[:, g * SSM_STATE:(g + 1) * SSM_STATE]
        gmat = _dot_nt(cg, bg.astype(BF16))
        st_old = st_ref[g]
        y_off = _dot(cg, st_old.astype(BF16)) * e_in[:, g * gw:(g + 1) * gw]
        pieces = []
        for j in range(hg // 2):
            halves = []
            for u in range(2):
                hh = g * hg + 2 * j + u
                seg = jnp.exp(jnp.where(causal, a_cum[:, hh:hh + 1] - a_cum_t[hh:hh + 1, :], NEG_INF))
                mh = (gmat * seg).astype(BF16)
                blk = (g * hg + 2 * j) * SSM_HEAD_DIM
                halves.append(_dot(mh, xdt_b[:, blk:blk + LANES]))
            pieces.append(jnp.where(lane_q < SSM_HEAD_DIM, halves[0], halves[1]))
        y_diag = jnp.concatenate(pieces, axis=1)
        y_groups.append(y_diag + y_off)
        st_ref[g] = (st_old * chunk_decay[:, g * gw:(g + 1) * gw]
                     + _dot(bg.T.astype(BF16), xdec_b[:, g * gw:(g + 1) * gw]))
    y = jnp.concatenate(y_groups, axis=1) + dsk_ref[...] * xs

    zv = z_ref[0]
    y = y * (zv * jax.nn.sigmoid(zv))
    outs = []
    for g in range(SSM_GROUPS):
        yg = y[:, g * gw:(g + 1) * gw]
        ms = jnp.mean(yg * yg, axis=-1, keepdims=True)
        outs.append((yg * lax.rsqrt(ms + EPS)) * nw_ref[:, g * gw:(g + 1) * gw])
    y_ref[0] = jnp.concatenate(outs, axis=1).astype(y_ref.dtype)

    tail = buf_ref[valid:valid + CONV_PAD, :]
    buf_ref[0:CONV_PAD, :] = tail

    @pl.when(c == n_c - 1)
    def _():
        convo_ref[0] = tail
        so_ref[0] = st_ref[...]


def _ssd(proj3, dt3, conv0, s0t, shared_init, valid, p, n_heads):
    B, L, _ = proj3.shape
    ssm_w = n_heads * SSM_HEAD_DIM
    gw = ssm_w // SSM_GROUPS
    conv_dim = ssm_w + 2 * SSM_GROUPS * SSM_STATE
    Q = min(L, 128)
    n_c = L // Q
    if valid < Q:
        assert n_c == 1
    else:
        valid = Q
    dtt3 = jnp.swapaxes(dt3[:, :, :n_heads], 1, 2)
    z_blk = 3 * ssm_w // ssm_w
    x_blk = 4 * ssm_w // gw
    init_idx = (lambda b, c: (0, 0, 0)) if shared_init else (lambda b, c: (b, 0, 0))
    init_idx4 = (lambda b, c: (0, 0, 0, 0)) if shared_init else (lambda b, c: (b, 0, 0, 0))
    const2 = lambda b, c: (0, 0)
    in_specs = [
        pl.BlockSpec((1, Q, ssm_w), lambda b, c: (b, c, z_blk)),
        pl.BlockSpec((1, Q, gw), lambda b, c: (b, c, x_blk)),
        pl.BlockSpec((1, Q, gw), lambda b, c: (b, c, x_blk + 1)),
        pl.BlockSpec((1, Q, gw), lambda b, c: (b, c, x_blk + 2)),
        pl.BlockSpec((1, Q, LANES), lambda b, c: (b, c, 0)),
        pl.BlockSpec((1, n_heads, Q), lambda b, c: (b, 0, c)),
        pl.BlockSpec((1, CONV_PAD, conv_dim), init_idx),
        pl.BlockSpec((1, SSM_GROUPS, SSM_STATE, gw), init_idx4),
        pl.BlockSpec((CONV_WIDTH, conv_dim), const2),
        pl.BlockSpec((1, conv_dim), const2),
        pl.BlockSpec((1, LANES), const2),
        pl.BlockSpec((n_heads, 1), const2),
        pl.BlockSpec((1, LANES), const2),
        pl.BlockSpec((n_heads, 1), const2),
        pl.BlockSpec((1, ssm_w), const2),
        pl.BlockSpec((1, ssm_w), const2),
        pl.BlockSpec((LANES, ssm_w), const2),
    ]
    out_specs = [
        pl.BlockSpec((1, Q, ssm_w), lambda b, c: (b, c, 0)),
        pl.BlockSpec((1, CONV_PAD, conv_dim), lambda b, c: (b, 0, 0)),
        pl.BlockSpec((1, SSM_GROUPS, SSM_STATE, gw), lambda b, c: (b, 0, 0, 0)),
    ]
    out_shape = [
        jax.ShapeDtypeStruct((B, L, ssm_w), BF16),
        jax.ShapeDtypeStruct((B, CONV_PAD, conv_dim), F32),
        jax.ShapeDtypeStruct((B, SSM_GROUPS, SSM_STATE, gw), F32),
    ]
    return pl.pallas_call(
        functools.partial(_ssd_kernel, Q=Q, valid=valid, n_heads=n_heads),
        grid=(B, n_c),
        in_specs=in_specs,
        out_specs=out_specs,
        out_shape=out_shape,
        scratch_shapes=[
            pltpu.VMEM((Q + CONV_PAD, conv_dim), F32),
            pltpu.VMEM((SSM_GROUPS, SSM_STATE, gw), F32),
        ],
        compiler_params=_cparams(("parallel", "arbitrary")),
        name="conv_ssd",
    )(proj3, proj3, proj3, proj3, dt3, dtt3, conv0, s0t,
      p["conv_w"], p["conv_b"], p["dt_bias"], p["dt_bias_t"], p["a_log"], p["a_log_t"],
      p["d_skip_x"], p["norm_w"], p["expand"])


def _state_to_t(s, n_heads):
    B = s.shape[0]
    hg = n_heads // SSM_GROUPS
    s = s.reshape(B, SSM_GROUPS, hg * SSM_HEAD_DIM, SSM_STATE)
    return jnp.swapaxes(s, 2, 3)


def _state_from_t(st, n_heads):
    B = st.shape[0]
    return jnp.swapaxes(st, 2, 3).reshape(B, n_heads, SSM_HEAD_DIM, SSM_STATE)


def _out_router_kernel(x_ref, oa_ref, os_ref, wo_ref, nw_ref, wr_ref, br_ref,
                       x1_ref, h2_ref, gate_ref, idx_ref, *, att_w):
    x1 = x_ref[...] + _dot(oa_ref[...], wo_ref[0:att_w, :]) + _dot(os_ref[...], wo_ref[att_w:, :])
    x1_ref[...] = x1
    ms = jnp.mean(x1 * x1, axis=-1, keepdims=True)
    h2 = ((x1 * lax.rsqrt(ms + EPS)) * nw_ref[...]).astype(BF16)
    h2_ref[...] = h2
    logits = _dot(h2, wr_ref[...]) + br_ref[...]
    lane = lax.broadcasted_iota(jnp.int32, logits.shape, 1)
    vals, idxs = [], []
    work = logits
    for _ in range(TOP_K):
        m = jnp.max(work, axis=-1, keepdims=True)
        i = jnp.min(jnp.where(work == m, lane, LANES), axis=-1, keepdims=True)
        vals.append(m)
        idxs.append(i)
        work = jnp.where(lane == i, NEG_INF, work)
    exps = [jnp.exp(v - vals[0]) for v in vals]
    denom = exps[0] + exps[1] + exps[2] + exps[3]
    gate = jnp.zeros(logits.shape, F32)
    idx = jnp.zeros(logits.shape, jnp.int32)
    for k in range(TOP_K):
        gate = jnp.where(lane == k, exps[k] / denom, gate)
        idx = jnp.where(lane == k, idxs[k], idx)
    gate_ref[...] = gate
    idx_ref[...] = idx


def _out_router(x2d, o_att, o_ssm, w_out_b, norm_w, w_router_b, b_router_p):
    T, D = x2d.shape
    att_w = o_att.shape[1]
    ssm_w = o_ssm.shape[1]
    bm = _tile(T, 512)
    row = lambda i: (i, 0)
    const = lambda i: (0, 0)
    return pl.pallas_call(
        functools.partial(_out_router_kernel, att_w=att_w),
        grid=(T // bm,),
        in_specs=[
            pl.BlockSpec((bm, D), row),
            pl.BlockSpec((bm, att_w), row),
            pl.BlockSpec((bm, ssm_w), row),
            pl.BlockSpec((att_w + ssm_w, D), const),
            pl.BlockSpec((1, D), const),
            pl.BlockSpec((D, LANES), const),
            pl.BlockSpec((1, LANES), const),
        ],
        out_specs=[
            pl.BlockSpec((bm, D), row),
            pl.BlockSpec((bm, D), row),
            pl.BlockSpec((bm, LANES), row),
            pl.BlockSpec((bm, LANES), row),
        ],
        out_shape=[
            jax.ShapeDtypeStruct((T, D), F32),
            jax.ShapeDtypeStruct((T, D), BF16),
            jax.ShapeDtypeStruct((T, LANES), F32),
            jax.ShapeDtypeStruct((T, LANES), jnp.int32),
        ],
        compiler_params=_cparams(("parallel",)),
        name="out_proj_router",
    )(x2d, o_att, o_ssm, w_out_b, norm_w.reshape(1, D), w_router_b, b_router_p)


def _moe_kernel(be_ref, nb_ref, x_ref, wg_ref, wu_ref, bg_ref, bu_ref, wd_ref, bd_ref, y_ref):
    i = pl.program_id(0)
    f = pl.program_id(1)

    @pl.when(i < nb_ref[0])
    def _():
        x = x_ref[...]
        gate = jnp.minimum(_dot(x, wg_ref[0].astype(BF16)) + bg_ref[0], SWIGLU_LIMIT)
        up = jnp.clip(_dot(x, wu_ref[0].astype(BF16)) + bu_ref[0], -SWIGLU_LIMIT, SWIGLU_LIMIT)
        act = (gate * jax.nn.sigmoid(SWIGLU_ALPHA * gate) * (up + 1.0)).astype(BF16)
        part = _dot(act, wd_ref[0].astype(BF16))

        @pl.when(f == 0)
        def _():
            y_ref[...] = part + bd_ref[0]

        @pl.when(f > 0)
        def _():
            y_ref[...] += part


def _moe_ffn_blocks(x_sorted, block_e, n_used, w_gate_up_l, b_gate_up_l, w_down_l, b_down_l, bm):
    R, D = x_sorted.shape
    E, _, two_f = w_gate_up_l.shape
    d_ff = two_f // 2
    tf = _tile(d_ff, 512)
    n_f = d_ff // tf
    n_blocks = R // bm
    bgu = b_gate_up_l.reshape(E, 1, two_f)
    bd = b_down_l.reshape(E, 1, D)
    grid_spec = pltpu.PrefetchScalarGridSpec(
        num_scalar_prefetch=2,
        grid=(n_blocks, n_f),
        in_specs=[
            pl.BlockSpec((bm, D), lambda i, f, be, nb: (i, 0)),
            pl.BlockSpec((1, D, tf), lambda i, f, be, nb: (be[i], 0, f)),
            pl.BlockSpec((1, D, tf), lambda i, f, be, nb: (be[i], 0, n_f + f)),
            pl.BlockSpec((1, 1, tf), lambda i, f, be, nb: (be[i], 0, f)),
            pl.BlockSpec((1, 1, tf), lambda i, f, be, nb: (be[i], 0, n_f + f)),
            pl.BlockSpec((1, tf, D), lambda i, f, be, nb: (be[i], f, 0)),
            pl.BlockSpec((1, 1, D), lambda i, f, be, nb: (be[i], 0, 0)),
        ],
        out_specs=pl.BlockSpec((bm, D), lambda i, f, be, nb: (i, 0)),
    )
    return pl.pallas_call(
        _moe_kernel,
        grid_spec=grid_spec,
        out_shape=jax.ShapeDtypeStruct((R, D), F32),
        compiler_params=_cparams(("parallel", "arbitrary")),
        name="moe_ffn",
    )(block_e, n_used, x_sorted, w_gate_up_l, w_gate_up_l, bgu, bgu, w_down_l, bd)


def _moe_route(top_e, n_experts, bm):
    T, K = top_e.shape
    M = T * K
    n_blocks = -(-M // bm) + n_experts
    flat_e = top_e.reshape(-1)
    order = jnp.argsort(flat_e)
    sorted_e = flat_e[order]
    counts = jnp.bincount(flat_e, length=n_experts)
    padded = ((counts + bm - 1) // bm) * bm
    start = jnp.cumsum(counts) - counts
    pend = jnp.cumsum(padded)
    pstart = pend - padded
    dest = (pstart[sorted_e] + jnp.arange(M) - start[sorted_e]).astype(jnp.int32)
    slot_tok = jnp.full((n_blocks * bm,), T, jnp.int32).at[dest].set((order // K).astype(jnp.int32))
    block_e = jnp.minimum(jnp.searchsorted(pend, jnp.arange(n_blocks) * bm, side="right"),
                          n_experts - 1).astype(jnp.int32)
    slot_of = jnp.zeros((M,), jnp.int32).at[order].set(dest)
    n_used = (pend[-1] // bm).astype(jnp.int32).reshape(1)
    last_e = block_e[jnp.maximum(n_used[0] - 1, 0)]
    block_e = jnp.where(jnp.arange(n_blocks) < n_used[0], block_e, last_e)
    return slot_tok, block_e, n_used, slot_of.reshape(T, K)


def _combine_kernel(x_ref, yg_ref, gate_ref, nw_ref, xo_ref, yo_ref):
    gate = gate_ref[...]
    y = x_ref[...]
    d = y.shape[1]
    for k in range(TOP_K):
        y = y + gate[:, k:k + 1] * yg_ref[:, k * d:(k + 1) * d]
    xo_ref[...] = y
    ms = jnp.mean(y * y, axis=-1, keepdims=True)
    yo_ref[...] = (y * lax.rsqrt(ms + EPS)) * nw_ref[...]


def _combine(x1, yg, gates, norm_w):
    T, D = x1.shape
    bm = _tile(T, 256)
    return pl.pallas_call(
        _combine_kernel,
        grid=(T // bm,),
        in_specs=[
            pl.BlockSpec((bm, D), lambda i: (i, 0)),
            pl.BlockSpec((bm, TOP_K * D), lambda i: (i, 0)),
            pl.BlockSpec((bm, LANES), lambda i: (i, 0)),
            pl.BlockSpec((1, D), lambda i: (0, 0)),
        ],
        out_specs=[pl.BlockSpec((bm, D), lambda i: (i, 0)), pl.BlockSpec((bm, D), lambda i: (i, 0))],
        out_shape=[jax.ShapeDtypeStruct((T, D), F32), jax.ShapeDtypeStruct((T, D), F32)],
        compiler_params=_cparams(("parallel",)),
        name="moe_combine_norm",
    )(x1, yg, gates, norm_w.reshape(1, D))


def kernel(x_prompt, x_sample, cache_k, cache_v, page_table, state_ssm, state_conv, meta_tokens,
           norm_mix_w, w_in, lambda_q1, lambda_k1, lambda_q2, lambda_k2, subln_w, conv_w, conv_b,
           dt_bias, a_log, d_skip, ssm_norm_w, w_out, norm_ffn_w, w_router, b_router,
           w_gate_up, b_gate_up, w_down, b_down, norm_final_w):
    depth = w_in.shape[0]
    Bp, Lp, D = x_prompt.shape
    Bs, Ss, _ = x_sample.shape
    n_heads_att = cache_v.shape[3]
    att_w = n_heads_att * ATT_V_DIM
    qk_w = n_heads_att * 2 * ATT_HEAD_DIM
    n_heads_ssm = state_ssm.shape[2]
    ssm_w = n_heads_ssm * SSM_HEAD_DIM
    conv_dim = state_conv.shape[-1]
    n_main = 2 * qk_w + att_w + ssm_w + conv_dim
    n_experts = w_router.shape[-1]
    n_pool, page = cache_k.shape[1], cache_k.shape[2]
    Tp, Ts = Bp * Lp, Bs * Ss
    Tsm = Ts + N_META
    chunk = 128

    xp = x_prompt.reshape(Tp, D)
    xsm = jnp.concatenate([x_sample.reshape(Ts, D), meta_tokens.astype(F32)], axis=0)

    outs = {k: [] for k in ("kp", "vp", "ks", "vs", "ssmp", "convp", "ssms", "convs")}
    for layer in range(depth):
        lam_init = _lambda_init(layer)
        lam_vec = jnp.stack([lambda_q1[layer], lambda_k1[layer], lambda_q2[layer], lambda_k2[layer]])
        w_main = w_in[layer][:, :n_main].astype(BF16)
        w_dt = jnp.pad(w_in[layer][:, n_main:], ((0, 0), (0, LANES - n_heads_ssm))).astype(BF16)
        pad_h = (0, LANES - n_heads_ssm)
        ssd_p = {
            "conv_w": conv_w[layer], "conv_b": conv_b[layer].reshape(1, conv_dim),
            "dt_bias": jnp.pad(dt_bias[layer], pad_h).reshape(1, LANES),
            "dt_bias_t": dt_bias[layer].reshape(n_heads_ssm, 1),
            "a_log": jnp.pad(a_log[layer], pad_h).reshape(1, LANES),
            "a_log_t": a_log[layer].reshape(n_heads_ssm, 1),
            "d_skip_x": jnp.repeat(d_skip[layer], SSM_HEAD_DIM).reshape(1, ssm_w),
            "norm_w": ssm_norm_w[layer].reshape(1, ssm_w),
            "expand": (jnp.arange(LANES)[:, None] == (jnp.arange(ssm_w) // SSM_HEAD_DIM)[None, :]).astype(BF16),
        }

        proj_p, dt_p = _rms_proj(xp, norm_mix_w[layer], w_main, w_dt)
        proj_sm, dt_sm = _rms_proj(xsm, norm_mix_w[layer], w_main, w_dt)
        proj_p3 = proj_p.reshape(Bp, Lp, n_main)
        proj_s3 = proj_sm[:Ts].reshape(Bs, Ss, n_main)
        proj_m = proj_sm[Ts:]

        oa_m = _flash_attn(proj_m[None], None, lam_vec, subln_w[layer], lam_init, n_heads_att)
        oa_p = _flash_attn(proj_p3, proj_m, lam_vec, subln_w[layer], lam_init, n_heads_att)
        ck = cache_k[layer].reshape(n_pool, page, qk_w)
        cv = cache_v[layer].reshape(n_pool, page, att_w)
        oa_s = _sample_attn(proj_s3, ck, cv, page_table, lam_vec, subln_w[layer], lam_init, n_heads_att)

        def pad_rows(a, rows):
            return jnp.pad(a, ((0, 0), (0, rows - a.shape[1]), (0, 0)))

        zero_conv = jnp.zeros((1, CONV_PAD, conv_dim), F32)
        zero_state = jnp.zeros((1, SSM_GROUPS, SSM_STATE, ssm_w // SSM_GROUPS), F32)
        ys_m, conv_m, st_m = _ssd(pad_rows(proj_m[None], chunk), pad_rows(dt_sm[Ts:][None], chunk),
                                  zero_conv, zero_state, True, N_META, ssd_p, n_heads_ssm)
        ys_p, conv_p, st_p = _ssd(proj_p3, dt_p.reshape(Bp, Lp, LANES), conv_m, st_m, True, chunk,
                                  ssd_p, n_heads_ssm)
        conv0_s = jnp.pad(state_conv[layer].astype(F32), ((0, 0), (CONV_PAD - (CONV_WIDTH - 1), 0), (0, 0)))
        ys_s, conv_s, st_s = _ssd(pad_rows(proj_s3, chunk), pad_rows(dt_sm[:Ts].reshape(Bs, Ss, LANES), chunk),
                                  conv0_s, _state_to_t(state_ssm[layer].astype(F32), n_heads_ssm),
                                  False, Ss, ssd_p, n_heads_ssm)

        w_out_b = w_out[layer].astype(BF16)
        w_router_b = jnp.pad(w_router[layer], ((0, 0), (0, LANES - n_experts))).astype(BF16)
        b_router_p = jnp.pad(b_router[layer].astype(F32), (0, LANES - n_experts),
                             constant_values=-1e30).reshape(1, LANES)
        oa_sm = jnp.concatenate([oa_s.reshape(Ts, att_w).astype(BF16), oa_m[0]], axis=0)
        os_sm = jnp.concatenate([ys_s[:, :Ss].reshape(Ts, ssm_w), ys_m[0, :N_META]], axis=0)
        x1_p, h2_p, gate_p, idx_p = _out_router(xp, oa_p.reshape(Tp, att_w), ys_p.reshape(Tp, ssm_w),
                                                w_out_b, norm_ffn_w[layer], w_router_b, b_router_p)
        x1_sm, h2_sm, gate_sm, idx_sm = _out_router(xsm, oa_sm, os_sm, w_out_b, norm_ffn_w[layer],
                                                    w_router_b, b_router_p)

        h2 = jnp.concatenate([h2_p, h2_sm], axis=0)
        top_e = jnp.concatenate([idx_p[:, :TOP_K], idx_sm[:, :TOP_K]], axis=0)
        T_all = Tp + Tsm
        bm_moe = 512 if T_all * TOP_K >= 512 * n_experts else 128
        slot_tok, block_e, n_used, slot_of = _moe_route(top_e, n_experts, bm_moe)
        h2_pad = jnp.concatenate([h2, jnp.zeros((1, D), BF16)], axis=0)
        x_sorted = h2_pad[slot_tok]
        yb = _moe_ffn_blocks(x_sorted, block_e, n_used, w_gate_up[layer], b_gate_up[layer],
                             w_down[layer], b_down[layer], bm_moe)
        yg = yb[slot_of].reshape(T_all, TOP_K * D)
        norm_last = norm_final_w if layer == depth - 1 else jnp.ones((D,), F32)
        xp, y_p = _combine(x1_p, yg[:Tp], gate_p, norm_last)
        xsm, y_sm = _combine(x1_sm, yg[Tp:], gate_sm, norm_last)

        k_m = jnp.broadcast_to(proj_m[None, :, qk_w:2 * qk_w], (Bp, N_META, qk_w))
        v_m = jnp.broadcast_to(proj_m[None, :, 2 * qk_w:2 * qk_w + att_w], (Bp, N_META, att_w))
        k_p = jnp.concatenate([k_m, proj_p3[:, :, qk_w:2 * qk_w]], axis=1)
        v_p = jnp.concatenate([v_m, proj_p3[:, :, 2 * qk_w:2 * qk_w + att_w]], axis=1)
        outs["kp"].append(k_p.reshape(Bp, N_META + Lp, n_heads_att, 2, ATT_HEAD_DIM))
        outs["vp"].append(v_p.reshape(Bp, N_META + Lp, n_heads_att, ATT_V_DIM))
        outs["ks"].append(proj_s3[:, :, qk_w:2 * qk_w].reshape(Bs, Ss, n_heads_att, 2, ATT_HEAD_DIM))
        outs["vs"].append(proj_s3[:, :, 2 * qk_w:2 * qk_w + att_w].reshape(Bs, Ss, n_heads_att, ATT_V_DIM))
        outs["ssmp"].append(_state_from_t(st_p, n_heads_ssm))
        outs["convp"].append(conv_p[:, CONV_PAD - (CONV_WIDTH - 1):])
        outs["ssms"].append(_state_from_t(st_s, n_heads_ssm))
        outs["convs"].append(conv_s[:, CONV_PAD - (CONV_WIDTH - 1):])

    y_prompt = y_p.reshape(Bp, Lp, D)
    y_sample = y_sm[:Ts].reshape(Bs, Ss, D)
    return (y_prompt, y_sample, jnp.stack(outs["kp"]), jnp.stack(outs["vp"]), jnp.stack(outs["ks"]),
            jnp.stack(outs["vs"]), jnp.stack(outs["ssmp"]), jnp.stack(outs["convp"]),
            jnp.stack(outs["ssms"]), jnp.stack(outs["convs"]))
```

```python
import functools
import math

import jax
import jax.numpy as jnp
from jax import lax
from jax.experimental import pallas as pl
from jax.experimental.pallas import tpu as pltpu

F32 = jnp.float32
BF16 = jnp.bfloat16

N_META = 16
ATT_HEAD_DIM = 64
ATT_V_DIM = 128
SSM_HEAD_DIM = 64
SSM_GROUPS = 2
SSM_STATE = 128
CONV_WIDTH = 4
TOP_K = 4
SWIGLU_ALPHA = 1.702
SWIGLU_LIMIT = 7.0
EPS = 1e-5
ATT_SCALE = ATT_HEAD_DIM ** -0.5

LANES = 128
SUBLANES = 8
VMEM_LIMIT_BYTES = 56 * 1024 * 1024
NEG_INF = float("-inf")
CONV_PAD = SUBLANES


def _tile(n, pref):
    if n <= pref:
        return n
    for t in range(pref, 0, -1):
        if n % t == 0 and t % SUBLANES == 0:
            return t
    return n


def _cparams(sem):
    return pltpu.CompilerParams(dimension_semantics=sem, vmem_limit_bytes=VMEM_LIMIT_BYTES)


def _lambda_init(layer):
    return 0.8 - 0.6 * math.exp(-0.3 * layer)


def _lam_from_ref(lam_ref, lam_init):
    lv = lam_ref[...]
    s1 = jnp.sum(lv[0:1] * lv[1:2], axis=-1, keepdims=True)
    s2 = jnp.sum(lv[2:3] * lv[3:4], axis=-1, keepdims=True)
    return jnp.exp(s1) - jnp.exp(s2) + lam_init


def _split3(x):
    hi = x.astype(BF16)
    r1 = x - hi.astype(F32)
    mid = r1.astype(BF16)
    lo = (r1 - mid.astype(F32)).astype(BF16)
    return hi, mid, lo


def _dot(a, b):
    return jnp.dot(a, b, preferred_element_type=F32)


def _dot_nt(a, b):
    return lax.dot_general(a, b, (((1,), (1,)), ((), ())), preferred_element_type=F32)


def _rms_proj_kernel(x_ref, nw_ref, w_ref, wdt_ref, proj_ref, dt_ref, h_ref):
    @pl.when(pl.program_id(1) == 0)
    def _():
        x = x_ref[...]
        ms = jnp.mean(x * x, axis=-1, keepdims=True)
        h = ((x * lax.rsqrt(ms + EPS)) * nw_ref[...]).astype(BF16)
        h_ref[...] = h
        dt_ref[...] = _dot(h, wdt_ref[...])

    proj_ref[...] = _dot(h_ref[...], w_ref[...])


def _rms_proj(x2d, norm_w, w_main, w_dt):
    T, D = x2d.shape
    N = w_main.shape[1]
    bm = _tile(T, 1024)
    bn = _tile(N, 512)
    return pl.pallas_call(
        _rms_proj_kernel,
        grid=(T // bm, N // bn),
        in_specs=[
            pl.BlockSpec((bm, D), lambda i, j: (i, 0)),
            pl.BlockSpec((1, D), lambda i, j: (0, 0)),
            pl.BlockSpec((D, bn), lambda i, j: (0, j)),
            pl.BlockSpec((D, LANES), lambda i, j: (0, 0)),
        ],
        out_specs=[
            pl.BlockSpec((bm, bn), lambda i, j: (i, j)),
            pl.BlockSpec((bm, LANES), lambda i, j: (i, 0)),
        ],
        out_shape=[jax.ShapeDtypeStruct((T, N), F32), jax.ShapeDtypeStruct((T, LANES), F32)],
        scratch_shapes=[pltpu.VMEM((bm, D), BF16)],
        compiler_params=_cparams(("parallel", "arbitrary")),
        name="rms_proj",
    )(x2d, norm_w.reshape(1, D), w_main, w_dt)


def _flash_kernel(*refs, tq, n_prefix, lam_init):
    if n_prefix:
        (qi_tab, ki_tab, lam_ref, sub_ref, q_ref, k_ref, v_ref, pk_ref, pv_ref,
         o_ref, m_ref, l_ref, acc_ref) = refs
    else:
        (qi_tab, ki_tab, lam_ref, sub_ref, q_ref, k_ref, v_ref,
         o_ref, m_ref, l_ref, acc_ref) = refs
    t = pl.program_id(2)
    qi = qi_tab[t]
    ki = ki_tab[t]

    q = q_ref[0] * ATT_SCALE
    lane = lax.broadcasted_iota(jnp.int32, q.shape, 1)
    q_maps = (jnp.where(lane < ATT_HEAD_DIM, q, 0.0).astype(BF16),
              jnp.where(lane >= ATT_HEAD_DIM, q, 0.0).astype(BF16))

    def update(k, v, mask):
        kb = k.astype(BF16)
        vb = v.astype(BF16)
        reps = k.shape[0] // LANES
        for c in range(2):
            s = _dot_nt(q_maps[c], kb)
            if mask is not None:
                s = jnp.where(mask, s, NEG_INF)
            m_prev = m_ref[c]
            m_new = jnp.maximum(m_prev, jnp.max(s, axis=-1, keepdims=True))
            corr = jnp.exp(m_prev - m_new)
            p = jnp.exp(s - pltpu.repeat(m_new, reps, axis=1))
            part = p[:, 0:LANES]
            for j in range(1, reps):
                part = part + p[:, j * LANES:(j + 1) * LANES]
            l_ref[c] = l_ref[c] * corr + part
            acc_ref[c] = acc_ref[c] * corr + _dot(p.astype(BF16), vb)
            m_ref[c] = m_new

    @pl.when(ki == 0)
    def _():
        m_ref[...] = jnp.full(m_ref.shape, NEG_INF, F32)
        l_ref[...] = jnp.zeros(l_ref.shape, F32)
        acc_ref[...] = jnp.zeros(acc_ref.shape, F32)
        if n_prefix:
            col = lax.broadcasted_iota(jnp.int32, (tq, pk_ref.shape[0]), 1)
            update(pk_ref[...], pv_ref[...], col < n_prefix)

    @pl.when(ki < qi)
    def _():
        update(k_ref[0], v_ref[0], None)

    @pl.when(ki == qi)
    def _():
        row = lax.broadcasted_iota(jnp.int32, (tq, tq), 0)
        col = lax.broadcasted_iota(jnp.int32, (tq, tq), 1)
        update(k_ref[0], v_ref[0], row >= col)
        lam = _lam_from_ref(lam_ref, lam_init)
        l0 = jnp.sum(l_ref[0], axis=-1, keepdims=True)
        l1 = jnp.sum(l_ref[1], axis=-1, keepdims=True)
        o = acc_ref[0] / l0 - lam * (acc_ref[1] / l1)
        ms = jnp.mean(o * o, axis=-1, keepdims=True)
        y = (o * lax.rsqrt(ms + EPS)) * sub_ref[...]
        o_ref[0] = (y * (1.0 - lam_init)).astype(o_ref.dtype)


def _flash_attn(proj3, prefix, n_prefix, lam_vec, subln_w, lam_init, n_heads, out_dtype=BF16):
    B, L, _ = proj3.shape
    assert L % LANES == 0
    tq = _tile(L, 512)
    nq = L // tq
    pairs = [(a, b) for a in range(nq) for b in range(a + 1)]
    qi_tab = jnp.asarray([p[0] for p in pairs], jnp.int32)
    ki_tab = jnp.asarray([p[1] for p in pairs], jnp.int32)
    H = n_heads
    has_prefix = prefix is not None
    if not has_prefix:
        n_prefix = 0
    in_specs = [
        pl.BlockSpec((4, ATT_HEAD_DIM), lambda b, h, t, qt, kt: (0, 0)),
        pl.BlockSpec((1, ATT_V_DIM), lambda b, h, t, qt, kt: (0, 0)),
        pl.BlockSpec((1, tq, LANES), lambda b, h, t, qt, kt: (b, qt[t], h)),
        pl.BlockSpec((1, tq, LANES), lambda b, h, t, qt, kt: (b, kt[t], H + h)),
        pl.BlockSpec((1, tq, LANES), lambda b, h, t, qt, kt: (b, kt[t], 2 * H + h)),
    ]
    args = [lam_vec, subln_w.reshape(1, ATT_V_DIM), proj3, proj3, proj3]
    if has_prefix:
        P = prefix.shape[0]
        in_specs += [
            pl.BlockSpec((P, LANES), lambda b, h, t, qt, kt: (0, H + h)),
            pl.BlockSpec((P, LANES), lambda b, h, t, qt, kt: (0, 2 * H + h)),
        ]
        args += [prefix, prefix]
    grid_spec = pltpu.PrefetchScalarGridSpec(
        num_scalar_prefetch=2,
        grid=(B, H, len(pairs)),
        in_specs=in_specs,
        out_specs=pl.BlockSpec((1, tq, LANES), lambda b, h, t, qt, kt: (b, qt[t], h)),
        scratch_shapes=[
            pltpu.VMEM((2, tq, LANES), F32),
            pltpu.VMEM((2, tq, LANES), F32),
            pltpu.VMEM((2, tq, ATT_V_DIM), F32),
        ],
    )
    return pl.pallas_call(
        functools.partial(_flash_kernel, tq=tq, n_prefix=n_prefix, lam_init=lam_init),
        grid_spec=grid_spec,
        out_shape=jax.ShapeDtypeStruct((B, L, H * ATT_V_DIM), out_dtype),
        compiler_params=_cparams(("parallel", "parallel", "arbitrary")),
        name="flash_diff_attn",
    )(qi_tab, ki_tab, *args)


def _sample_attn_kernel(*refs, pps, n_heads, s_new, page, lam_init):
    pt_ref, lam_ref, sub_ref, q_ref, kn_ref, vn_ref = refs[:6]
    k_refs = refs[6:6 + pps]
    v_refs = refs[6 + pps:6 + 2 * pps]
    o_ref, qbd_ref, m_ref, l_ref, acc_ref = refs[6 + 2 * pps:]
    step = pl.program_id(1)
    n_rows = n_heads * 2 * s_new
    width = n_heads * 2 * ATT_HEAD_DIM

    def update(ss, v_head):
        m_prev = m_ref[...]
        m_new = m_prev
        for s in ss:
            m_new = jnp.maximum(m_new, jnp.max(s, axis=-1, keepdims=True))
        corr = jnp.exp(m_prev - m_new)
        ps = [jnp.exp(s - m_new) for s in ss]
        l_new = l_ref[...] * corr
        for p in ps:
            l_new = l_new + jnp.sum(p, axis=-1, keepdims=True)
        l_ref[...] = l_new
        m_ref[...] = m_new
        pbs = [p.astype(BF16) for p in ps]
        for h in range(n_heads):
            r0, r1 = h * 2 * s_new, (h + 1) * 2 * s_new
            pv = None
            for j, pb in enumerate(pbs):
                d = _dot(pb[r0:r1, :], v_head(j, h))
                pv = d if pv is None else pv + d
            acc_ref[r0:r1, :] = acc_ref[r0:r1, :] * corr[r0:r1] + pv

    @pl.when(step == 0)
    def _():
        q = q_ref[0] * ATT_SCALE
        qt = jnp.concatenate([q] * (2 * n_heads), axis=0)
        row_hc = lax.broadcasted_iota(jnp.int32, (n_rows, width), 0) // s_new
        col_hc = lax.broadcasted_iota(jnp.int32, (n_rows, width), 1) // ATT_HEAD_DIM
        qbd = jnp.where(row_hc == col_hc, qt, 0.0).astype(BF16)
        qbd_ref[...] = qbd
        m_ref[...] = jnp.full(m_ref.shape, NEG_INF, F32)
        l_ref[...] = jnp.zeros(l_ref.shape, F32)
        acc_ref[...] = jnp.zeros(acc_ref.shape, F32)
        pad = jnp.zeros((page - s_new, width), F32)
        k_self = jnp.concatenate([kn_ref[0], pad], axis=0).astype(BF16)
        v_self = jnp.concatenate([vn_ref[0], pad], axis=0).astype(BF16)
        r = lax.broadcasted_iota(jnp.int32, (n_rows, page), 0)
        c = lax.broadcasted_iota(jnp.int32, (n_rows, page), 1)
        s_self = jnp.where(c <= (r % s_new), _dot_nt(qbd, k_self), NEG_INF)
        update([s_self], lambda j, h: v_self[:, h * ATT_V_DIM:(h + 1) * ATT_V_DIM])

    qbd = qbd_ref[...]
    update([_dot(qbd, k[0].astype(BF16)) for k in k_refs],
           lambda j, h: v_refs[j][0, pl.ds(h, page, stride=n_heads), :].astype(BF16))

    @pl.when(step == pl.num_programs(1) - 1)
    def _():
        lam = _lam_from_ref(lam_ref, lam_init)
        o_all = acc_ref[...] / l_ref[...]
        for h in range(n_heads):
            r0 = h * 2 * s_new
            o = o_all[r0:r0 + s_new] - lam * o_all[r0 + s_new:r0 + 2 * s_new]
            ms = jnp.mean(o * o, axis=-1, keepdims=True)
            y = (o * lax.rsqrt(ms + EPS)) * sub_ref[...]
            o_ref[0, :, h * ATT_V_DIM:(h + 1) * ATT_V_DIM] = y * (1.0 - lam_init)


def _sample_attn(proj_s3, cache_k_l, cache_v_l, page_table, lam_vec, subln_w, lam_init, n_heads):
    b, S, _ = proj_s3.shape
    n_pool, width, page = cache_k_l.shape
    n_pages = page_table.shape[1]
    pps = 8 if n_pages % 8 == 0 else 1
    n_rows = n_heads * 2 * S

    def k_spec(j):
        return pl.BlockSpec((1, width, page), lambda i, p, pt: (pt[i, p * pps + j], 0, 0))

    def v_spec(j):
        return pl.BlockSpec((1, page * n_heads, ATT_V_DIM), lambda i, p, pt: (pt[i, p * pps + j], 0, 0))

    in_specs = [
        pl.BlockSpec((4, ATT_HEAD_DIM), lambda i, p, pt: (0, 0)),
        pl.BlockSpec((1, ATT_V_DIM), lambda i, p, pt: (0, 0)),
        pl.BlockSpec((1, S, width), lambda i, p, pt: (i, 0, 0)),
        pl.BlockSpec((1, S, width), lambda i, p, pt: (i, 0, 1)),
        pl.BlockSpec((1, S, width), lambda i, p, pt: (i, 0, 2)),
    ] + [k_spec(j) for j in range(pps)] + [v_spec(j) for j in range(pps)]
    grid_spec = pltpu.PrefetchScalarGridSpec(
        num_scalar_prefetch=1,
        grid=(b, n_pages // pps),
        in_specs=in_specs,
        out_specs=pl.BlockSpec((1, S, width), lambda i, p, pt: (i, 0, 0)),
        scratch_shapes=[
            pltpu.VMEM((n_rows, width), BF16),
            pltpu.VMEM((n_rows, 1), F32),
            pltpu.VMEM((n_rows, 1), F32),
            pltpu.VMEM((n_rows, ATT_V_DIM), F32),
        ],
    )
    return pl.pallas_call(
        functools.partial(_sample_attn_kernel, pps=pps, n_heads=n_heads, s_new=S, page=page,
                          lam_init=lam_init),
        grid_spec=grid_spec,
        out_shape=jax.ShapeDtypeStruct((b, S, width), F32),
        compiler_params=_cparams(("parallel", "arbitrary")),
        name="sample_paged_attn",
    )(page_table, lam_vec, subln_w.reshape(1, ATT_V_DIM), proj_s3, proj_s3, proj_s3,
      *([cache_k_l] * pps), *([cache_v_l] * pps))


def _ssd_kernel(z_ref, xa_ref, xb_ref, bc_ref, dt_ref, dtt_ref, conv0_ref, s0_ref,
                cw_ref, cb_ref, dtb_ref, dtbt_ref, alog_ref, alogt_ref, dsk_ref, nw_ref, ex_ref,
                y_ref, convo_ref, so_ref, buf_ref, st_ref, *, Q, valid, n_heads):
    c = pl.program_id(1)
    n_c = pl.num_programs(1)
    ssm_w = n_heads * SSM_HEAD_DIM
    gw = ssm_w // SSM_GROUPS
    hg = n_heads // SSM_GROUPS

    @pl.when(c == 0)
    def _():
        buf_ref[0:CONV_PAD, :] = conv0_ref[0]
        st_ref[...] = s0_ref[0]

    buf_ref[CONV_PAD:CONV_PAD + Q, 0:gw] = xa_ref[0]
    buf_ref[CONV_PAD:CONV_PAD + Q, gw:2 * gw] = xb_ref[0]
    buf_ref[CONV_PAD:CONV_PAD + Q, 2 * gw:3 * gw] = bc_ref[0]
    base = CONV_PAD - (CONV_WIDTH - 1)
    conv = cb_ref[...]
    for w in range(CONV_WIDTH):
        conv = conv + buf_ref[base + w:base + w + Q, :] * cw_ref[w:w + 1, :]
    xbc = conv * jax.nn.sigmoid(conv)
    xs = xbc[:, :ssm_w]
    bmat = xbc[:, ssm_w:ssm_w + SSM_GROUPS * SSM_STATE]
    cmat = xbc[:, ssm_w + SSM_GROUPS * SSM_STATE:]

    lane = lax.broadcasted_iota(jnp.int32, (1, LANES), 1)
    a_row = jnp.where(lane < n_heads, -jnp.exp(alog_ref[...]), 0.0)
    dt = jax.nn.softplus(dt_ref[0] + dtb_ref[...])
    a_col = -jnp.exp(alogt_ref[...])
    dtt = jax.nn.softplus(dtt_ref[0] + dtbt_ref[...])
    if valid < Q:
        dt = jnp.where(lax.broadcasted_iota(jnp.int32, dt.shape, 0) < valid, dt, 0.0)
        dtt = jnp.where(lax.broadcasted_iota(jnp.int32, dtt.shape, 1) < valid, dtt, 0.0)
    da = dt * a_row
    dat = dtt * a_col

    r_i = lax.broadcasted_iota(jnp.int32, (Q, Q), 0)
    c_i = lax.broadcasted_iota(jnp.int32, (Q, Q), 1)
    causal = r_i >= c_i
    tri_l = jnp.where(causal, 1.0, 0.0).astype(BF16)
    tri_u = jnp.where(r_i <= c_i, 1.0, 0.0).astype(BF16)
    da3 = _split3(da)
    a_cum = _dot(tri_l, da3[0]) + _dot(tri_l, da3[1]) + _dot(tri_l, da3[2])
    dat3 = _split3(dat)
    a_cum_t = _dot(dat3[0], tri_u) + _dot(dat3[1], tri_u) + _dot(dat3[2], tri_u)

    ex = ex_ref[...]
    dt3 = _split3(dt)
    dt_x = _dot(dt3[0], ex) + _dot(dt3[1], ex) + _dot(dt3[2], ex)
    ac3 = _split3(a_cum)
    acum_x = _dot(ac3[0], ex) + _dot(ac3[1], ex) + _dot(ac3[2], ex)
    a_last_x = acum_x[Q - 1:Q, :]
    e_in = jnp.exp(acum_x)
    dec_end = jnp.exp(a_last_x - acum_x)
    chunk_decay = jnp.exp(a_last_x)

    xdt = xs * dt_x
    xdt_b = xdt.astype(BF16)
    xdec_b = (xdt * dec_end).astype(BF16)
    lane_q = lax.broadcasted_iota(jnp.int32, (Q, LANES), 1)

    y_groups = []
    for g in range(SSM_GROUPS):
        cg = cmat[:, g * SSM_STATE:(g + 1) * SSM_STATE].astype(BF16)
        bg = bmat[:, g * SSM_STATE:(g + 1) * SSM_STATE]
        gmat = _dot_nt(cg, bg.astype(BF16))
        st_old = st_ref[g]
        y_off = _dot(cg, st_old.astype(BF16)) * e_in[:, g * gw:(g + 1) * gw]
        pieces = []
        for j in range(hg // 2):
            halves = []
            for u in range(2):
                hh = g * hg + 2 * j + u
                seg = jnp.exp(jnp.where(causal, a_cum[:, hh:hh + 1] - a_cum_t[hh:hh + 1, :], NEG_INF))
                mh = (gmat * seg).astype(BF16)
                blk = (g * hg + 2 * j) * SSM_HEAD_DIM
                halves.append(_dot(mh, xdt_b[:, blk:blk + LANES]))
            pieces.append(jnp.where(lane_q < SSM_HEAD_DIM, halves[0], halves[1]))
        y_diag = jnp.concatenate(pieces, axis=1)
        y_groups.append(y_diag + y_off)
        st_ref[g] = (st_old * chunk_decay[:, g * gw:(g + 1) * gw]
                     + _dot(bg.T.astype(BF16), xdec_b[:, g * gw:(g + 1) * gw]))
    y = jnp.concatenate(y_groups, axis=1) + dsk_ref[...] * xs

    zv = z_ref[0]
    y = y * (zv * jax.nn.sigmoid(zv))
    outs = []
    for g in range(SSM_GROUPS):
        yg = y[:, g * gw:(g + 1) * gw]
        ms = jnp.mean(yg * yg, axis=-1, keepdims=True)
        outs.append((yg * lax.rsqrt(ms + EPS)) * nw_ref[:, g * gw:(g + 1) * gw])
    y_ref[0] = jnp.concatenate(outs, axis=1).astype(y_ref.dtype)

    tail = buf_ref[valid:valid + CONV_PAD, :]
    buf_ref[0:CONV_PAD, :] = tail

    @pl.when(c == n_c - 1)
    def _():
        convo_ref[0] = tail
        so_ref[0] = st_ref[...]


def _ssd(proj3, dt3, conv0, s0t, shared_init, valid, p, n_heads):
    B, L, _ = proj3.shape
    ssm_w = n_heads * SSM_HEAD_DIM
    gw = ssm_w // SSM_GROUPS
    conv_dim = ssm_w + 2 * SSM_GROUPS * SSM_STATE
    Q = min(L, 128)
    n_c = L // Q
    if valid < Q:
        assert n_c == 1
    else:
        valid = Q
    dtt3 = jnp.swapaxes(dt3[:, :, :n_heads], 1, 2)
    z_blk = 3 * ssm_w // ssm_w
    x_blk = 4 * ssm_w // gw
    init_idx = (lambda b, c: (0, 0, 0)) if shared_init else (lambda b, c: (b, 0, 0))
    init_idx4 = (lambda b, c: (0, 0, 0, 0)) if shared_init else (lambda b, c: (b, 0, 0, 0))
    const2 = lambda b, c: (0, 0)
    in_specs = [
        pl.BlockSpec((1, Q, ssm_w), lambda b, c: (b, c, z_blk)),
        pl.BlockSpec((1, Q, gw), lambda b, c: (b, c, x_blk)),
        pl.BlockSpec((1, Q, gw), lambda b, c: (b, c, x_blk + 1)),
        pl.BlockSpec((1, Q, gw), lambda b, c: (b, c, x_blk + 2)),
        pl.BlockSpec((1, Q, LANES), lambda b, c: (b, c, 0)),
        pl.BlockSpec((1, n_heads, Q), lambda b, c: (b, 0, c)),
        pl.BlockSpec((1, CONV_PAD, conv_dim), init_idx),
        pl.BlockSpec((1, SSM_GROUPS, SSM_STATE, gw), init_idx4),
        pl.BlockSpec((CONV_WIDTH, conv_dim), const2),
        pl.BlockSpec((1, conv_dim), const2),
        pl.BlockSpec((1, LANES), const2),
        pl.BlockSpec((n_heads, 1), const2),
        pl.BlockSpec((1, LANES), const2),
        pl.BlockSpec((n_heads, 1), const2),
        pl.BlockSpec((1, ssm_w), const2),
        pl.BlockSpec((1, ssm_w), const2),
        pl.BlockSpec((LANES, ssm_w), const2),
    ]
    out_specs = [
        pl.BlockSpec((1, Q, ssm_w), lambda b, c: (b, c, 0)),
        pl.BlockSpec((1, CONV_PAD, conv_dim), lambda b, c: (b, 0, 0)),
        pl.BlockSpec((1, SSM_GROUPS, SSM_STATE, gw), lambda b, c: (b, 0, 0, 0)),
    ]
    out_shape = [
        jax.ShapeDtypeStruct((B, L, ssm_w), BF16),
        jax.ShapeDtypeStruct((B, CONV_PAD, conv_dim), F32),
        jax.ShapeDtypeStruct((B, SSM_GROUPS, SSM_STATE, gw), F32),
    ]
    return pl.pallas_call(
        functools.partial(_ssd_kernel, Q=Q, valid=valid, n_heads=n_heads),
        grid=(B, n_c),
        in_specs=in_specs,
        out_specs=out_specs,
        out_shape=out_shape,
        scratch_shapes=[
            pltpu.VMEM((Q + CONV_PAD, conv_dim), F32),
            pltpu.VMEM((SSM_GROUPS, SSM_STATE, gw), F32),
        ],
        compiler_params=_cparams(("parallel", "arbitrary")),
        name="conv_ssd",
    )(proj3, proj3, proj3, proj3, dt3, dtt3, conv0, s0t,
      p["conv_w"], p["conv_b"], p["dt_bias"], p["dt_bias_t"], p["a_log"], p["a_log_t"],
      p["d_skip_x"], p["norm_w"], p["expand"])


def _state_to_t(s, n_heads):
    B = s.shape[0]
    hg = n_heads // SSM_GROUPS
    s = s.reshape(B, SSM_GROUPS, hg * SSM_HEAD_DIM, SSM_STATE)
    return jnp.swapaxes(s, 2, 3)


def _state_from_t(st, n_heads):
    B = st.shape[0]
    return jnp.swapaxes(st, 2, 3).reshape(B, n_heads, SSM_HEAD_DIM, SSM_STATE)


def _out_router_kernel(x_ref, oa_ref, os_ref, wo_ref, nw_ref, wr_ref, br_ref,
                       x1_ref, h2_ref, gate_ref, idx_ref, *, att_w):
    x1 = x_ref[...] + _dot(oa_ref[...], wo_ref[0:att_w, :]) + _dot(os_ref[...], wo_ref[att_w:, :])
    x1_ref[...] = x1
    ms = jnp.mean(x1 * x1, axis=-1, keepdims=True)
    h2 = ((x1 * lax.rsqrt(ms + EPS)) * nw_ref[...]).astype(BF16)
    half = h2.shape[1] // 2
    h2f = h2.astype(F32)
    lo_bits = lax.bitcast_convert_type(h2f[:, :half], jnp.uint32)
    hi_bits = lax.bitcast_convert_type(h2f[:, half:], jnp.uint32)
    h2_ref[...] = hi_bits | lax.shift_right_logical(lo_bits, jnp.uint32(16))
    logits = _dot(h2, wr_ref[...]) + br_ref[...]
    lane = lax.broadcasted_iota(jnp.int32, logits.shape, 1)
    vals, idxs = [], []
    work = logits
    for _ in range(TOP_K):
        m = jnp.max(work, axis=-1, keepdims=True)
        i = jnp.min(jnp.where(work == m, lane, LANES), axis=-1, keepdims=True)
        vals.append(m)
        idxs.append(i)
        work = jnp.where(lane == i, NEG_INF, work)
    exps = [jnp.exp(v - vals[0]) for v in vals]
    denom = exps[0] + exps[1] + exps[2] + exps[3]
    gate = jnp.zeros(logits.shape, F32)
    idx = jnp.zeros(logits.shape, jnp.int32)
    for k in range(TOP_K):
        gate = jnp.where(lane == k, exps[k] / denom, gate)
        idx = jnp.where(lane == k, idxs[k], idx)
    gate_ref[...] = gate
    idx_ref[...] = idx


def _out_router(x2d, o_att, o_ssm, w_out_b, norm_w, w_router_b, b_router_p):
    T, D = x2d.shape
    att_w = o_att.shape[1]
    ssm_w = o_ssm.shape[1]
    bm = _tile(T, 512)
    row = lambda i: (i, 0)
    const = lambda i: (0, 0)
    return pl.pallas_call(
        functools.partial(_out_router_kernel, att_w=att_w),
        grid=(T // bm,),
        in_specs=[
            pl.BlockSpec((bm, D), row),
            pl.BlockSpec((bm, att_w), row),
            pl.BlockSpec((bm, ssm_w), row),
            pl.BlockSpec((att_w + ssm_w, D), const),
            pl.BlockSpec((1, D), const),
            pl.BlockSpec((D, LANES), const),
            pl.BlockSpec((1, LANES), const),
        ],
        out_specs=[
            pl.BlockSpec((bm, D), row),
            pl.BlockSpec((bm, D // 2), row),
            pl.BlockSpec((bm, LANES), row),
            pl.BlockSpec((bm, LANES), row),
        ],
        out_shape=[
            jax.ShapeDtypeStruct((T, D), F32),
            jax.ShapeDtypeStruct((T, D // 2), jnp.uint32),
            jax.ShapeDtypeStruct((T, LANES), F32),
            jax.ShapeDtypeStruct((T, LANES), jnp.int32),
        ],
        compiler_params=_cparams(("parallel",)),
        name="out_proj_router",
    )(x2d, o_att, o_ssm, w_out_b, norm_w.reshape(1, D), w_router_b, b_router_p)


def _moe_kernel(be_ref, br_ref, x_ref, wg_ref, wu_ref, bg_ref, bu_ref, wd_ref, bd_ref, y_ref, xb_ref,
                *, sub):
    i = pl.program_id(0)
    f = pl.program_id(1)
    rows = br_ref[i]
    bm, half = x_ref.shape

    @pl.when((f == 0) & (rows > 0))
    def _():
        xp = x_ref[...]
        lo = lax.bitcast_convert_type(lax.shift_left(xp, jnp.uint32(16)), F32)
        hi = lax.bitcast_convert_type(xp & jnp.uint32(0xFFFF0000), F32)
        xb_ref[:, :half] = lo.astype(BF16)
        xb_ref[:, half:] = hi.astype(BF16)

    for r in range(bm // sub):
        rs = slice(r * sub, (r + 1) * sub)

        @pl.when(f == 0)
        def _():
            live = jnp.where(rows > r * sub, 1.0, 0.0)
            y_ref[rs, :] = jnp.broadcast_to(bd_ref[0] * live, (sub, y_ref.shape[1]))

        @pl.when(rows > r * sub)
        def _():
            x = xb_ref[rs, :]
            gate = jnp.minimum(_dot(x, wg_ref[0].astype(BF16)) + bg_ref[0], SWIGLU_LIMIT)
            up = jnp.clip(_dot(x, wu_ref[0].astype(BF16)) + bu_ref[0], -SWIGLU_LIMIT, SWIGLU_LIMIT)
            act = (gate * jax.nn.sigmoid(SWIGLU_ALPHA * gate) * (up + 1.0)).astype(BF16)
            y_ref[rs, :] += _dot(act, wd_ref[0].astype(BF16))


def _moe_ffn_blocks(x_sorted, block_e, block_rows, w_gate_up_l, b_gate_up_l, w_down_l, b_down_l, bm):
    R, half = x_sorted.shape
    D = 2 * half
    E, _, two_f = w_gate_up_l.shape
    d_ff = two_f // 2
    tf = _tile(d_ff, 256)
    n_f = d_ff // tf
    n_blocks = R // bm
    sub = bm // 4
    bgu =b_gate_up_l.reshape(E, 1, two_f)
    bd = b_down_l.reshape(E, 1, D)
    grid_spec = pltpu.PrefetchScalarGridSpec(
        num_scalar_prefetch=2,
        grid=(n_blocks, n_f),
        in_specs=[
            pl.BlockSpec((bm, half), lambda i, f, be, br: (i, 0)),
            pl.BlockSpec((1, D, tf), lambda i, f, be, br: (be[i], 0, f)),
            pl.BlockSpec((1, D, tf), lambda i, f, be, br: (be[i], 0, n_f + f)),
            pl.BlockSpec((1, 1, tf), lambda i, f, be, br: (be[i], 0, f)),
            pl.BlockSpec((1, 1, tf), lambda i, f, be, br: (be[i], 0, n_f + f)),
            pl.BlockSpec((1, tf, D), lambda i, f, be, br: (be[i], f, 0)),
            pl.BlockSpec((1, 1, D), lambda i, f, be, br: (be[i], 0, 0)),
        ],
        out_specs=pl.BlockSpec((bm, D), lambda i, f, be, br: (i, 0)),
        scratch_shapes=[pltpu.VMEM((bm, D), BF16)],
    )
    return pl.pallas_call(
        functools.partial(_moe_kernel, sub=sub),
        grid_spec=grid_spec,
        out_shape=jax.ShapeDtypeStruct((R, D), F32),
        compiler_params=_cparams(("parallel", "arbitrary")),
        name="moe_ffn",
    )(block_e, block_rows, x_sorted, w_gate_up_l, w_gate_up_l, bgu, bgu, w_down_l, bd)


def _moe_route(top_e, n_experts, bm):
    T, K = top_e.shape
    M = T * K
    n_blocks = (M + n_experts * (bm - 1)) // bm
    flat_e = top_e.reshape(-1)
    order = jnp.argsort(flat_e)
    sorted_e = flat_e[order]
    counts = jnp.bincount(flat_e, length=n_experts)
    padded = ((counts + bm - 1) // bm) * bm
    start = jnp.cumsum(counts) - counts
    pend = jnp.cumsum(padded)
    pstart = pend - padded
    dest = (pstart[sorted_e] + jnp.arange(M) - start[sorted_e]).astype(jnp.int32)
    slot_tok = jnp.full((n_blocks * bm,), T, jnp.int32).at[dest].set((order // K).astype(jnp.int32))
    block_e = jnp.minimum(jnp.searchsorted(pend, jnp.arange(n_blocks) * bm, side="right"),
                          n_experts - 1).astype(jnp.int32)
    slot_of = jnp.zeros((M,), jnp.int32).at[order].set(dest)
    n_used = pend[-1] // bm
    blk = jnp.arange(n_blocks)
    block_rows = jnp.clip(counts[block_e] - (blk * bm - pstart[block_e]), 0, bm)
    block_rows = jnp.where(blk < n_used, block_rows, 0).astype(jnp.int32)
    last_e = block_e[jnp.maximum(n_used - 1, 0)]
    block_e = jnp.where(blk < n_used, block_e, last_e)
    return slot_tok, block_e, block_rows, slot_of.reshape(T, K)


def _combine_kernel(x_ref, *refs):
    yg_refs = refs[:TOP_K]
    gate_ref, nw_ref, xo_ref, yo_ref = refs[TOP_K:]
    gate = gate_ref[...]
    y = x_ref[...]
    for k in range(TOP_K):
        y = y + gate[:, k:k + 1] * yg_refs[k][0]
    xo_ref[...] = y
    ms = jnp.mean(y * y, axis=-1, keepdims=True)
    yo_ref[...] = (y * lax.rsqrt(ms + EPS)) * nw_ref[...]


def _combine(x1, yg, row0, gates, norm_w):
    T, D = x1.shape
    bm = _tile(math.gcd(T, row0) if row0 else T, 256)
    off = row0 // bm

    def yg_spec(k):
        return pl.BlockSpec((1, bm, D), lambda i: (k, off + i, 0))

    return pl.pallas_call(
        _combine_kernel,
        grid=(T // bm,),
        in_specs=[pl.BlockSpec((bm, D), lambda i: (i, 0))] + [yg_spec(k) for k in range(TOP_K)] + [
            pl.BlockSpec((bm, LANES), lambda i: (i, 0)),
            pl.BlockSpec((1, D), lambda i: (0, 0)),
        ],
        out_specs=[pl.BlockSpec((bm, D), lambda i: (i, 0)), pl.BlockSpec((bm, D), lambda i: (i, 0))],
        out_shape=[jax.ShapeDtypeStruct((T, D), F32), jax.ShapeDtypeStruct((T, D), F32)],
        compiler_params=_cparams(("parallel",)),
        name="moe_combine_norm",
    )(x1, *([yg] * TOP_K), gates, norm_w.reshape(1, D))


def kernel(x_prompt, x_sample, cache_k, cache_v, page_table, state_ssm, state_conv, meta_tokens,
           norm_mix_w, w_in, lambda_q1, lambda_k1, lambda_q2, lambda_k2, subln_w, conv_w, conv_b,
           dt_bias, a_log, d_skip, ssm_norm_w, w_out, norm_ffn_w, w_router, b_router,
           w_gate_up, b_gate_up, w_down, b_down, norm_final_w):
    depth = w_in.shape[0]
    Bp, Lp, D = x_prompt.shape
    Bs, Ss, _ = x_sample.shape
    n_heads_att = cache_v.shape[3]
    att_w = n_heads_att * ATT_V_DIM
    qk_w = n_heads_att * 2 * ATT_HEAD_DIM
    n_heads_ssm = state_ssm.shape[2]
    ssm_w = n_heads_ssm * SSM_HEAD_DIM
    conv_dim = state_conv.shape[-1]
    n_main = 2 * qk_w + att_w + ssm_w + conv_dim
    n_experts = w_router.shape[-1]
    n_pool, page = cache_k.shape[1], cache_k.shape[2]
    Tp, Ts = Bp * Lp, Bs * Ss
    Tsm = Ts + N_META
    chunk = 128

    xp = x_prompt.reshape(Tp, D)
    xsm = jnp.concatenate([x_sample.reshape(Ts, D), meta_tokens.astype(F32)], axis=0)

    outs = {k: [] for k in ("kp", "vp", "ks", "vs", "ssmp", "convp", "ssms", "convs")}
    for layer in range(depth):
        lam_init = _lambda_init(layer)
        lam_vec = jnp.stack([lambda_q1[layer], lambda_k1[layer], lambda_q2[layer], lambda_k2[layer]])
        w_main = w_in[layer][:, :n_main].astype(BF16)
        w_dt = jnp.pad(w_in[layer][:, n_main:], ((0, 0), (0, LANES - n_heads_ssm))).astype(BF16)
        pad_h = (0, LANES - n_heads_ssm)
        ssd_p = {
            "conv_w": conv_w[layer], "conv_b": conv_b[layer].reshape(1, conv_dim),
            "dt_bias": jnp.pad(dt_bias[layer], pad_h).reshape(1, LANES),
            "dt_bias_t": dt_bias[layer].reshape(n_heads_ssm, 1),
            "a_log": jnp.pad(a_log[layer], pad_h).reshape(1, LANES),
            "a_log_t": a_log[layer].reshape(n_heads_ssm, 1),
            "d_skip_x": jnp.repeat(d_skip[layer], SSM_HEAD_DIM).reshape(1, ssm_w),
            "norm_w": ssm_norm_w[layer].reshape(1, ssm_w),
            "expand": (jnp.arange(LANES)[:, None] == (jnp.arange(ssm_w) // SSM_HEAD_DIM)[None, :]).astype(BF16),
        }

        proj_p, dt_p = _rms_proj(xp, norm_mix_w[layer], w_main, w_dt)
        proj_sm, dt_sm = _rms_proj(xsm, norm_mix_w[layer], w_main, w_dt)
        proj_p3 = proj_p.reshape(Bp, Lp, n_main)
        proj_s3 = proj_sm[:Ts].reshape(Bs, Ss, n_main)
        proj_m = proj_sm[Ts:]

        def pad_rows(a, rows):
            return jnp.pad(a, ((0, 0), (0, rows - a.shape[1]), (0, 0)))

        proj_m_pad = pad_rows(proj_m[None], chunk)
        oa_m = _flash_attn(proj_m_pad, None, 0, lam_vec, subln_w[layer], lam_init, n_heads_att)
        oa_p = _flash_attn(proj_p3, proj_m_pad[0], N_META, lam_vec, subln_w[layer], lam_init, n_heads_att)
        ck = jnp.transpose(cache_k[layer], (0, 2, 3, 4, 1)).reshape(n_pool, qk_w, page)
        cv = cache_v[layer].reshape(n_pool, page * n_heads_att, ATT_V_DIM)
        oa_s = _sample_attn(proj_s3, ck, cv, page_table, lam_vec, subln_w[layer], lam_init, n_heads_att)

        zero_conv = jnp.zeros((1, CONV_PAD, conv_dim), F32)
        zero_state = jnp.zeros((1, SSM_GROUPS, SSM_STATE, ssm_w // SSM_GROUPS), F32)
        ys_m, conv_m, st_m = _ssd(proj_m_pad, pad_rows(dt_sm[Ts:][None], chunk),
                                  zero_conv, zero_state, True, N_META, ssd_p, n_heads_ssm)
        ys_p, conv_p, st_p = _ssd(proj_p3, dt_p.reshape(Bp, Lp, LANES), conv_m, st_m, True, chunk,
                                  ssd_p, n_heads_ssm)
        conv0_s = jnp.pad(state_conv[layer].astype(F32), ((0, 0), (CONV_PAD - (CONV_WIDTH - 1), 0), (0, 0)))
        ys_s, conv_s, st_s = _ssd(pad_rows(proj_s3, chunk), pad_rows(dt_sm[:Ts].reshape(Bs, Ss, LANES), chunk),
                                  conv0_s, _state_to_t(state_ssm[layer].astype(F32), n_heads_ssm),
                                  False, Ss, ssd_p, n_heads_ssm)

        w_out_b = w_out[layer].astype(BF16)
        w_router_b = jnp.pad(w_router[layer], ((0, 0), (0, LANES - n_experts))).astype(BF16)
        b_router_p = jnp.pad(b_router[layer].astype(F32), (0, LANES - n_experts),
                             constant_values=-1e30).reshape(1, LANES)
        oa_sm = jnp.concatenate([oa_s.reshape(Ts, att_w).astype(BF16), oa_m[0, :N_META]], axis=0)
        os_sm = jnp.concatenate([ys_s[:, :Ss].reshape(Ts, ssm_w), ys_m[0, :N_META]], axis=0)
        x1_p, h2_p, gate_p, idx_p = _out_router(xp, oa_p.reshape(Tp, att_w), ys_p.reshape(Tp, ssm_w),
                                                w_out_b, norm_ffn_w[layer], w_router_b, b_router_p)
        x1_sm, h2_sm, gate_sm, idx_sm = _out_router(xsm, oa_sm, os_sm, w_out_b, norm_ffn_w[layer],
                                                    w_router_b, b_router_p)

        h2 = jnp.concatenate([h2_p, h2_sm], axis=0)
        top_e = jnp.concatenate([idx_p[:, :TOP_K], idx_sm[:, :TOP_K]], axis=0)
        T_all = Tp + Tsm
        bm_moe = 1024 if T_all * TOP_K >= 1024 * n_experts else 256
        slot_tok, block_e, block_rows, slot_of = _moe_route(top_e, n_experts, bm_moe)
        h2_pad = jnp.concatenate([h2, jnp.zeros((1, D // 2), jnp.uint32)], axis=0)
        x_sorted = h2_pad[slot_tok]
        yb = _moe_ffn_blocks(x_sorted, block_e, block_rows, w_gate_up[layer], b_gate_up[layer],
                             w_down[layer], b_down[layer], bm_moe)
        yg = yb[slot_of.T.reshape(-1)].reshape(TOP_K, T_all, D)
        norm_last = norm_final_w if layer == depth - 1 else jnp.ones((D,), F32)
        xp, y_p = _combine(x1_p, yg, 0, gate_p, norm_last)
        xsm, y_sm = _combine(x1_sm, yg, Tp, gate_sm, norm_last)

        k_m = jnp.broadcast_to(proj_m[None, :, qk_w:2 * qk_w], (Bp, N_META, qk_w))
        v_m = jnp.broadcast_to(proj_m[None, :, 2 * qk_w:2 * qk_w + att_w], (Bp, N_META, att_w))
        k_p = jnp.concatenate([k_m, proj_p3[:, :, qk_w:2 * qk_w]], axis=1)
        v_p = jnp.concatenate([v_m, proj_p3[:, :, 2 * qk_w:2 * qk_w + att_w]], axis=1)
        outs["kp"].append(k_p.reshape(Bp, N_META + Lp, n_heads_att, 2, ATT_HEAD_DIM))
        outs["vp"].append(v_p.reshape(Bp, N_META + Lp, n_heads_att, ATT_V_DIM))
        outs["ks"].append(proj_s3[:, :, qk_w:2 * qk_w].reshape(Bs, Ss, n_heads_att, 2, ATT_HEAD_DIM))
        outs["vs"].append(proj_s3[:, :, 2 * qk_w:2 * qk_w + att_w].reshape(Bs, Ss, n_heads_att, ATT_V_DIM))
        outs["ssmp"].append(_state_from_t(st_p, n_heads_ssm))
        outs["convp"].append(conv_p[:, CONV_PAD - (CONV_WIDTH - 1):])
        outs["ssms"].append(_state_from_t(st_s, n_heads_ssm))
        outs["convs"].append(conv_s[:, CONV_PAD - (CONV_WIDTH - 1):])

    y_prompt = y_p.reshape(Bp, Lp, D)
    y_sample = y_sm[:Ts].reshape(Bs, Ss, D)
    return (y_prompt, y_sample, jnp.stack(outs["kp"]), jnp.stack(outs["vp"]), jnp.stack(outs["ks"]),
            jnp.stack(outs["vs"]), jnp.stack(outs["ssmp"]), jnp.stack(outs["convp"]),
            jnp.stack(outs["ssms"]), jnp.stack(outs["convs"]))
```

```python
import functools
import math

import jax
import jax.numpy as jnp
from jax import lax
from jax.experimental import pallas as pl
from jax.experimental.pallas import tpu as pltpu

F32 = jnp.float32
BF16 = jnp.bfloat16

N_META = 16
ATT_HEAD_DIM = 64
ATT_V_DIM = 128
SSM_HEAD_DIM = 64
SSM_GROUPS = 2
SSM_STATE = 128
CONV_WIDTH = 4
TOP_K = 4
SWIGLU_ALPHA = 1.702
SWIGLU_LIMIT = 7.0
EPS = 1e-5
ATT_SCALE = ATT_HEAD_DIM ** -0.5

LANES = 128
SUBLANES = 8
VMEM_LIMIT_BYTES = 56 * 1024 * 1024
NEG_INF = float("-inf")
CONV_PAD = SUBLANES


def _tile(n, pref):
    if n <= pref:
        return n
    for t in range(pref, 0, -1):
        if n % t == 0 and t % SUBLANES == 0:
            return t
    return n


def _cparams(sem):
    return pltpu.CompilerParams(dimension_semantics=sem, vmem_limit_bytes=VMEM_LIMIT_BYTES)


def _lambda_init(layer):
    return 0.8 - 0.6 * math.exp(-0.3 * layer)


def _lam_from_ref(lam_ref, lam_init):
    lv = lam_ref[...]
    s1 = jnp.sum(lv[0:1] * lv[1:2], axis=-1, keepdims=True)
    s2 = jnp.sum(lv[2:3] * lv[3:4], axis=-1, keepdims=True)
    return jnp.exp(s1) - jnp.exp(s2) + lam_init


def _split3(x):
    hi = x.astype(BF16)
    r1 = x - hi.astype(F32)
    mid = r1.astype(BF16)
    lo = (r1 - mid.astype(F32)).astype(BF16)
    return hi, mid, lo


def _dot(a, b):
    return jnp.dot(a, b, preferred_element_type=F32)


def _dot_nt(a, b):
    return lax.dot_general(a, b, (((1,), (1,)), ((), ())), preferred_element_type=F32)


def _rms_proj_kernel(x_ref, nw_ref, w_ref, wdt_ref, proj_ref, dt_ref, h_ref):
    @pl.when(pl.program_id(1) == 0)
    def _():
        x = x_ref[...]
        ms = jnp.mean(x * x, axis=-1, keepdims=True)
        h = ((x * lax.rsqrt(ms + EPS)) * nw_ref[...]).astype(BF16)
        h_ref[...] = h
        dt_ref[...] = _dot(h, wdt_ref[...])

    proj_ref[...] = _dot(h_ref[...], w_ref[...])


def _rms_proj(x2d, norm_w, w_main, w_dt):
    T, D = x2d.shape
    N = w_main.shape[1]
    bm = _tile(T, 1024)
    bn = _tile(N, 512)
    return pl.pallas_call(
        _rms_proj_kernel,
        grid=(T // bm, N // bn),
        in_specs=[
            pl.BlockSpec((bm, D), lambda i, j: (i, 0)),
            pl.BlockSpec((1, D), lambda i, j: (0, 0)),
            pl.BlockSpec((D, bn), lambda i, j: (0, j)),
            pl.BlockSpec((D, LANES), lambda i, j: (0, 0)),
        ],
        out_specs=[
            pl.BlockSpec((bm, bn), lambda i, j: (i, j)),
            pl.BlockSpec((bm, LANES), lambda i, j: (i, 0)),
        ],
        out_shape=[jax.ShapeDtypeStruct((T, N), F32), jax.ShapeDtypeStruct((T, LANES), F32)],
        scratch_shapes=[pltpu.VMEM((bm, D), BF16)],
        compiler_params=_cparams(("parallel", "arbitrary")),
        name="rms_proj",
    )(x2d, norm_w.reshape(1, D), w_main, w_dt)


def _flash_kernel(*refs, tq, n_prefix, lam_init):
    if n_prefix:
        (qi_tab, ki_tab, lam_ref, sub_ref, q_ref, k_ref, v_ref, pk_ref, pv_ref,
         o_ref, m_ref, l_ref, acc_ref) = refs
    else:
        (qi_tab, ki_tab, lam_ref, sub_ref, q_ref, k_ref, v_ref,
         o_ref, m_ref, l_ref, acc_ref) = refs
    t = pl.program_id(2)
    qi = qi_tab[t]
    ki = ki_tab[t]

    q = q_ref[0] * ATT_SCALE
    lane = lax.broadcasted_iota(jnp.int32, q.shape, 1)
    q_maps = (jnp.where(lane < ATT_HEAD_DIM, q, 0.0).astype(BF16),
              jnp.where(lane >= ATT_HEAD_DIM, q, 0.0).astype(BF16))

    def update(k, v, mask):
        kb = k.astype(BF16)
        vb = v.astype(BF16)
        reps = k.shape[0] // LANES
        for c in range(2):
            s = _dot_nt(q_maps[c], kb)
            if mask is not None:
                s = jnp.where(mask, s, NEG_INF)
            m_prev = m_ref[c]
            m_new = jnp.maximum(m_prev, jnp.max(s, axis=-1, keepdims=True))
            corr = jnp.exp(m_prev - m_new)
            p = jnp.exp(s - jnp.concatenate([m_new] * reps, axis=1))
            part = p[:, 0:LANES]
            for j in range(1, reps):
                part = part + p[:, j * LANES:(j + 1) * LANES]
            l_ref[c] = l_ref[c] * corr + part
            acc_ref[c] = acc_ref[c] * corr + _dot(p.astype(BF16), vb)
            m_ref[c] = m_new

    @pl.when(ki == 0)
    def _():
        m_ref[...] = jnp.full(m_ref.shape, NEG_INF, F32)
        l_ref[...] = jnp.zeros(l_ref.shape, F32)
        acc_ref[...] = jnp.zeros(acc_ref.shape, F32)
        if n_prefix:
            col = lax.broadcasted_iota(jnp.int32, (tq, pk_ref.shape[0]), 1)
            update(pk_ref[...], pv_ref[...], col < n_prefix)

    @pl.when(ki < qi)
    def _():
        update(k_ref[0], v_ref[0], None)

    @pl.when(ki == qi)
    def _():
        row = lax.broadcasted_iota(jnp.int32, (tq, tq), 0)
        col = lax.broadcasted_iota(jnp.int32, (tq, tq), 1)
        update(k_ref[0], v_ref[0], row >= col)
        lam = _lam_from_ref(lam_ref, lam_init)
        l0 = jnp.sum(l_ref[0], axis=-1, keepdims=True)
        l1 = jnp.sum(l_ref[1], axis=-1, keepdims=True)
        o = acc_ref[0] / l0 - lam * (acc_ref[1] / l1)
        ms = jnp.mean(o * o, axis=-1, keepdims=True)
        y = (o * lax.rsqrt(ms + EPS)) * sub_ref[...]
        o_ref[0] = (y * (1.0 - lam_init)).astype(o_ref.dtype)


def _flash_attn(proj3, prefix, n_prefix, lam_vec, subln_w, lam_init, n_heads, out_dtype=BF16):
    B, L, _ = proj3.shape
    assert L % LANES == 0
    tq = _tile(L, 512)
    nq = L // tq
    pairs = [(a, b) for a in range(nq) for b in range(a + 1)]
    qi_tab = jnp.asarray([p[0] for p in pairs], jnp.int32)
    ki_tab = jnp.asarray([p[1] for p in pairs], jnp.int32)
    H = n_heads
    has_prefix = prefix is not None
    if not has_prefix:
        n_prefix = 0
    in_specs = [
        pl.BlockSpec((4, ATT_HEAD_DIM), lambda b, h, t, qt, kt: (0, 0)),
        pl.BlockSpec((1, ATT_V_DIM), lambda b, h, t, qt, kt: (0, 0)),
        pl.BlockSpec((1, tq, LANES), lambda b, h, t, qt, kt: (b, qt[t], h)),
        pl.BlockSpec((1, tq, LANES), lambda b, h, t, qt, kt: (b, kt[t], H + h)),
        pl.BlockSpec((1, tq, LANES), lambda b, h, t, qt, kt: (b, kt[t], 2 * H + h)),
    ]
    args = [lam_vec, subln_w.reshape(1, ATT_V_DIM), proj3, proj3, proj3]
    if has_prefix:
        P = prefix.shape[0]
        in_specs += [
            pl.BlockSpec((P, LANES), lambda b, h, t, qt, kt: (0, H + h)),
            pl.BlockSpec((P, LANES), lambda b, h, t, qt, kt: (0, 2 * H + h)),
        ]
        args += [prefix, prefix]
    grid_spec = pltpu.PrefetchScalarGridSpec(
        num_scalar_prefetch=2,
        grid=(B, H, len(pairs)),
        in_specs=in_specs,
        out_specs=pl.BlockSpec((1, tq, LANES), lambda b, h, t, qt, kt: (b, qt[t], h)),
        scratch_shapes=[
            pltpu.VMEM((2, tq, LANES), F32),
            pltpu.VMEM((2, tq, LANES), F32),
            pltpu.VMEM((2, tq, ATT_V_DIM), F32),
        ],
    )
    return pl.pallas_call(
        functools.partial(_flash_kernel, tq=tq, n_prefix=n_prefix, lam_init=lam_init),
        grid_spec=grid_spec,
        out_shape=jax.ShapeDtypeStruct((B, L, H * ATT_V_DIM), out_dtype),
        compiler_params=_cparams(("parallel", "parallel", "arbitrary")),
        name="flash_diff_attn",
    )(qi_tab, ki_tab, *args)


def _sample_attn_kernel(*refs, pps, n_heads, s_new, page, lam_init):
    pt_ref, lam_ref, sub_ref, q_ref, kn_ref, vn_ref = refs[:6]
    k_refs = refs[6:6 + pps]
    v_refs = refs[6 + pps:6 + 2 * pps]
    o_ref, qbd_ref, m_ref, l_ref, acc_ref = refs[6 + 2 * pps:]
    step = pl.program_id(1)
    n_rows = n_heads * 2 * s_new
    width = n_heads * 2 * ATT_HEAD_DIM

    def update(ss, v_head):
        m_prev = m_ref[...]
        m_new = m_prev
        for s in ss:
            m_new = jnp.maximum(m_new, jnp.max(s, axis=-1, keepdims=True))
        corr = jnp.exp(m_prev - m_new)
        ps = [jnp.exp(s - m_new) for s in ss]
        l_new = l_ref[...] * corr
        for p in ps:
            l_new = l_new + jnp.sum(p, axis=-1, keepdims=True)
        l_ref[...] = l_new
        m_ref[...] = m_new
        pbs = [p.astype(BF16) for p in ps]
        for h in range(n_heads):
            r0, r1 = h * 2 * s_new, (h + 1) * 2 * s_new
            pv = None
            for j, pb in enumerate(pbs):
                d = _dot(pb[r0:r1, :], v_head(j, h))
                pv = d if pv is None else pv + d
            acc_ref[r0:r1, :] = acc_ref[r0:r1, :] * corr[r0:r1] + pv

    @pl.when(step == 0)
    def _():
        q = q_ref[0] * ATT_SCALE
        qt = jnp.concatenate([q] * (2 * n_heads), axis=0)
        row_hc = lax.broadcasted_iota(jnp.int32, (n_rows, width), 0) // s_new
        col_hc = lax.broadcasted_iota(jnp.int32, (n_rows, width), 1) // ATT_HEAD_DIM
        qbd = jnp.where(row_hc == col_hc, qt, 0.0).astype(BF16)
        qbd_ref[...] = qbd
        m_ref[...] = jnp.full(m_ref.shape, NEG_INF, F32)
        l_ref[...] = jnp.zeros(l_ref.shape, F32)
        acc_ref[...] = jnp.zeros(acc_ref.shape, F32)
        pad = jnp.zeros((page - s_new, width), F32)
        k_self = jnp.concatenate([kn_ref[0], pad], axis=0).astype(BF16)
        v_self = jnp.concatenate([vn_ref[0], pad], axis=0).astype(BF16)
        r = lax.broadcasted_iota(jnp.int32, (n_rows, page), 0)
        c = lax.broadcasted_iota(jnp.int32, (n_rows, page), 1)
        s_self = jnp.where(c <= (r % s_new), _dot_nt(qbd, k_self), NEG_INF)
        update([s_self], lambda j, h: v_self[:, h * ATT_V_DIM:(h + 1) * ATT_V_DIM])

    qbd = qbd_ref[...]
    update([_dot(qbd, k[0].astype(BF16)) for k in k_refs],
           lambda j, h: v_refs[j][0, pl.ds(h, page, stride=n_heads), :].astype(BF16))

    @pl.when(step == pl.num_programs(1) - 1)
    def _():
        lam = _lam_from_ref(lam_ref, lam_init)
        o_all = acc_ref[...] / l_ref[...]
        for h in range(n_heads):
            r0 = h * 2 * s_new
            o = o_all[r0:r0 + s_new] - lam * o_all[r0 + s_new:r0 + 2 * s_new]
            ms = jnp.mean(o * o, axis=-1, keepdims=True)
            y = (o * lax.rsqrt(ms + EPS)) * sub_ref[...]
            o_ref[0, :, h * ATT_V_DIM:(h + 1) * ATT_V_DIM] = y * (1.0 - lam_init)


def _sample_attn(proj_s3, cache_k_l, cache_v_l, page_table, lam_vec, subln_w, lam_init, n_heads):
    b, S, _ = proj_s3.shape
    n_pool, width, page = cache_k_l.shape
    n_pages = page_table.shape[1]
    pps = 8 if n_pages % 8 == 0 else 1
    n_rows = n_heads * 2 * S

    def k_spec(j):
        return pl.BlockSpec((1, width, page), lambda i, p, pt: (pt[i, p * pps + j], 0, 0))

    def v_spec(j):
        return pl.BlockSpec((1, page * n_heads, ATT_V_DIM), lambda i, p, pt: (pt[i, p * pps + j], 0, 0))

    in_specs = [
        pl.BlockSpec((4, ATT_HEAD_DIM), lambda i, p, pt: (0, 0)),
        pl.BlockSpec((1, ATT_V_DIM), lambda i, p, pt: (0, 0)),
        pl.BlockSpec((1, S, width), lambda i, p, pt: (i, 0, 0)),
        pl.BlockSpec((1, S, width), lambda i, p, pt: (i, 0, 1)),
        pl.BlockSpec((1, S, width), lambda i, p, pt: (i, 0, 2)),
    ] + [k_spec(j) for j in range(pps)] + [v_spec(j) for j in range(pps)]
    grid_spec = pltpu.PrefetchScalarGridSpec(
        num_scalar_prefetch=1,
        grid=(b, n_pages // pps),
        in_specs=in_specs,
        out_specs=pl.BlockSpec((1, S, width), lambda i, p, pt: (i, 0, 0)),
        scratch_shapes=[
            pltpu.VMEM((n_rows, width), BF16),
            pltpu.VMEM((n_rows, 1), F32),
            pltpu.VMEM((n_rows, 1), F32),
            pltpu.VMEM((n_rows, ATT_V_DIM), F32),
        ],
    )
    return pl.pallas_call(
        functools.partial(_sample_attn_kernel, pps=pps, n_heads=n_heads, s_new=S, page=page,
                          lam_init=lam_init),
        grid_spec=grid_spec,
        out_shape=jax.ShapeDtypeStruct((b, S, width), F32),
        compiler_params=_cparams(("parallel", "arbitrary")),
        name="sample_paged_attn",
    )(page_table, lam_vec, subln_w.reshape(1, ATT_V_DIM), proj_s3, proj_s3, proj_s3,
      *([cache_k_l] * pps), *([cache_v_l] * pps))


def _ssd_kernel(z_ref, xa_ref, xb_ref, bc_ref, dt_ref, dtt_ref, conv0_ref, s0_ref,
                cw_ref, cb_ref, dtb_ref, dtbt_ref, alog_ref, alogt_ref, dsk_ref, nw_ref, ex_ref,
                y_ref, convo_ref, so_ref, buf_ref, st_ref, *, Q, valid, n_heads):
    c = pl.program_id(1)
    n_c = pl.num_programs(1)
    ssm_w = n_heads * SSM_HEAD_DIM
    gw = ssm_w // SSM_GROUPS
    hg = n_heads // SSM_GROUPS

    @pl.when(c == 0)
    def _():
        buf_ref[0:CONV_PAD, :] = conv0_ref[0]
        st_ref[...] = s0_ref[0]

    buf_ref[CONV_PAD:CONV_PAD + Q, 0:gw] = xa_ref[0]
    buf_ref[CONV_PAD:CONV_PAD + Q, gw:2 * gw] = xb_ref[0]
    buf_ref[CONV_PAD:CONV_PAD + Q, 2 * gw:3 * gw] = bc_ref[0]
    base = CONV_PAD - (CONV_WIDTH - 1)
    conv = cb_ref[...]
    for w in range(CONV_WIDTH):
        conv = conv + buf_ref[base + w:base + w + Q, :] * cw_ref[w:w + 1, :]
    xbc = conv * jax.nn.sigmoid(conv)
    xs = xbc[:, :ssm_w]
    bmat = xbc[:, ssm_w:ssm_w + SSM_GROUPS * SSM_STATE]
    cmat = xbc[:, ssm_w + SSM_GROUPS * SSM_STATE:]

    lane = lax.broadcasted_iota(jnp.int32, (1, LANES), 1)
    a_row = jnp.where(lane < n_heads, -jnp.exp(alog_ref[...]), 0.0)
    dt = jax.nn.softplus(dt_ref[0] + dtb_ref[...])
    a_col = -jnp.exp(alogt_ref[...])
    dtt = jax.nn.softplus(dtt_ref[0] + dtbt_ref[...])
    if valid < Q:
        dt = jnp.where(lax.broadcasted_iota(jnp.int32, dt.shape, 0) < valid, dt, 0.0)
        dtt = jnp.where(lax.broadcasted_iota(jnp.int32, dtt.shape, 1) < valid, dtt, 0.0)
    da = dt * a_row
    dat = dtt * a_col

    r_i = lax.broadcasted_iota(jnp.int32, (Q, Q), 0)
    c_i = lax.broadcasted_iota(jnp.int32, (Q, Q), 1)
    causal = r_i >= c_i
    tri_l = jnp.where(causal, 1.0, 0.0).astype(BF16)
    tri_u = jnp.where(r_i <= c_i, 1.0, 0.0).astype(BF16)
    da3 = _split3(da)
    a_cum = _dot(tri_l, da3[0]) + _dot(tri_l, da3[1]) + _dot(tri_l, da3[2])
    dat3 = _split3(dat)
    a_cum_t = _dot(dat3[0], tri_u) + _dot(dat3[1], tri_u) + _dot(dat3[2], tri_u)

    ex = ex_ref[...]
    dt3 = _split3(dt)
    dt_x = _dot(dt3[0], ex) + _dot(dt3[1], ex) + _dot(dt3[2], ex)
    ac3 = _split3(a_cum)
    acum_x = _dot(ac3[0], ex) + _dot(ac3[1], ex) + _dot(ac3[2], ex)
    a_last_x = acum_x[Q - 1:Q, :]
    e_in = jnp.exp(acum_x)
    dec_end = jnp.exp(a_last_x - acum_x)
    chunk_decay = jnp.exp(a_last_x)

    xdt = xs * dt_x
    xdt_b = xdt.astype(BF16)
    xdec_b = (xdt * dec_end).astype(BF16)
    lane_q = lax.broadcasted_iota(jnp.int32, (Q, LANES), 1)

    y_groups = []
    for g in range(SSM_GROUPS):
        cg = cmat[:, g * SSM_STATE:(g + 1) * SSM_STATE].astype(BF16)
        bg = bmat[:, g * SSM_STATE:(g + 1) * SSM_STATE]
        gmat = _dot_nt(cg, bg.astype(BF16))
        st_old = st_ref[g]
        y_off = _dot(cg, st_old.astype(BF16)) * e_in[:, g * gw:(g + 1) * gw]
        pieces = []
        for j in range(hg // 2):
            halves = []
            for u in range(2):
                hh = g * hg + 2 * j + u
                seg = jnp.exp(jnp.where(causal, a_cum[:, hh:hh + 1] - a_cum_t[hh:hh + 1, :], NEG_INF))
                mh = (gmat * seg).astype(BF16)
                blk = (g * hg + 2 * j) * SSM_HEAD_DIM
                halves.append(_dot(mh, xdt_b[:, blk:blk + LANES]))
            pieces.append(jnp.where(lane_q < SSM_HEAD_DIM, halves[0], halves[1]))
        y_diag = jnp.concatenate(pieces, axis=1)
        y_groups.append(y_diag + y_off)
        st_ref[g] = (st_old * chunk_decay[:, g * gw:(g + 1) * gw]
                     + _dot(bg.T.astype(BF16), xdec_b[:, g * gw:(g + 1) * gw]))
    y = jnp.concatenate(y_groups, axis=1) + dsk_ref[...] * xs

    zv = z_ref[0]
    y = y * (zv * jax.nn.sigmoid(zv))
    outs = []
    for g in range(SSM_GROUPS):
        yg = y[:, g * gw:(g + 1) * gw]
        ms = jnp.mean(yg * yg, axis=-1, keepdims=True)
        outs.append((yg * lax.rsqrt(ms + EPS)) * nw_ref[:, g * gw:(g + 1) * gw])
    y_ref[0] = jnp.concatenate(outs, axis=1).astype(y_ref.dtype)

    tail = buf_ref[valid:valid + CONV_PAD, :]
    buf_ref[0:CONV_PAD, :] = tail

    @pl.when(c == n_c - 1)
    def _():
        convo_ref[0] = tail
        so_ref[0] = st_ref[...]


def _ssd(proj3, dt3, conv0, s0t, shared_init, valid, p, n_heads):
    B, L, _ = proj3.shape
    ssm_w = n_heads * SSM_HEAD_DIM
    gw = ssm_w // SSM_GROUPS
    conv_dim = ssm_w + 2 * SSM_GROUPS * SSM_STATE
    Q = min(L, 128)
    n_c = L // Q
    if valid < Q:
        assert n_c == 1
    else:
        valid = Q
    dtt3 = jnp.swapaxes(dt3[:, :, :n_heads], 1, 2)
    z_blk = 3 * ssm_w // ssm_w
    x_blk = 4 * ssm_w // gw
    init_idx = (lambda b, c: (0, 0, 0)) if shared_init else (lambda b, c: (b, 0, 0))
    init_idx4 = (lambda b, c: (0, 0, 0, 0)) if shared_init else (lambda b, c: (b, 0, 0, 0))
    const2 = lambda b, c: (0, 0)
    in_specs = [
        pl.BlockSpec((1, Q, ssm_w), lambda b, c: (b, c, z_blk)),
        pl.BlockSpec((1, Q, gw), lambda b, c: (b, c, x_blk)),
        pl.BlockSpec((1, Q, gw), lambda b, c: (b, c, x_blk + 1)),
        pl.BlockSpec((1, Q, gw), lambda b, c: (b, c, x_blk + 2)),
        pl.BlockSpec((1, Q, LANES), lambda b, c: (b, c, 0)),
        pl.BlockSpec((1, n_heads, Q), lambda b, c: (b, 0, c)),
        pl.BlockSpec((1, CONV_PAD, conv_dim), init_idx),
        pl.BlockSpec((1, SSM_GROUPS, SSM_STATE, gw), init_idx4),
        pl.BlockSpec((CONV_WIDTH, conv_dim), const2),
        pl.BlockSpec((1, conv_dim), const2),
        pl.BlockSpec((1, LANES), const2),
        pl.BlockSpec((n_heads, 1), const2),
        pl.BlockSpec((1, LANES), const2),
        pl.BlockSpec((n_heads, 1), const2),
        pl.BlockSpec((1, ssm_w), const2),
        pl.BlockSpec((1, ssm_w), const2),
        pl.BlockSpec((LANES, ssm_w), const2),
    ]
    out_specs = [
        pl.BlockSpec((1, Q, ssm_w), lambda b, c: (b, c, 0)),
        pl.BlockSpec((1, CONV_PAD, conv_dim), lambda b, c: (b, 0, 0)),
        pl.BlockSpec((1, SSM_GROUPS, SSM_STATE, gw), lambda b, c: (b, 0, 0, 0)),
    ]
    out_shape = [
        jax.ShapeDtypeStruct((B, L, ssm_w), BF16),
        jax.ShapeDtypeStruct((B, CONV_PAD, conv_dim), F32),
        jax.ShapeDtypeStruct((B, SSM_GROUPS, SSM_STATE, gw), F32),
    ]
    return pl.pallas_call(
        functools.partial(_ssd_kernel, Q=Q, valid=valid, n_heads=n_heads),
        grid=(B, n_c),
        in_specs=in_specs,
        out_specs=out_specs,
        out_shape=out_shape,
        scratch_shapes=[
            pltpu.VMEM((Q + CONV_PAD, conv_dim), F32),
            pltpu.VMEM((SSM_GROUPS, SSM_STATE, gw), F32),
        ],
        compiler_params=_cparams(("parallel", "arbitrary")),
        name="conv_ssd",
    )(proj3, proj3, proj3, proj3, dt3, dtt3, conv0, s0t,
      p["conv_w"], p["conv_b"], p["dt_bias"], p["dt_bias_t"], p["a_log"], p["a_log_t"],
      p["d_skip_x"], p["norm_w"], p["expand"])


def _state_to_t(s, n_heads):
    B = s.shape[0]
    hg = n_heads // SSM_GROUPS
    s = s.reshape(B, SSM_GROUPS, hg * SSM_HEAD_DIM, SSM_STATE)
    return jnp.swapaxes(s, 2, 3)


def _state_from_t(st, n_heads):
    B = st.shape[0]
    return jnp.swapaxes(st, 2, 3).reshape(B, n_heads, SSM_HEAD_DIM, SSM_STATE)


def _out_router_kernel(x_ref, oa_ref, os_ref, wo_ref, nw_ref, wr_ref, br_ref,
                       x1_ref, h2_ref, gate_ref, idx_ref, *, att_w):
    x1 = x_ref[...] + _dot(oa_ref[...], wo_ref[0:att_w, :]) + _dot(os_ref[...], wo_ref[att_w:, :])
    x1_ref[...] = x1
    ms = jnp.mean(x1 * x1, axis=-1, keepdims=True)
    h2 = ((x1 * lax.rsqrt(ms + EPS)) * nw_ref[...]).astype(BF16)
    half = h2.shape[1] // 2
    h2f = h2.astype(F32)
    lo_bits = lax.bitcast_convert_type(h2f[:, :half], jnp.uint32)
    hi_bits = lax.bitcast_convert_type(h2f[:, half:], jnp.uint32)
    h2_ref[...] = hi_bits | lax.shift_right_logical(lo_bits, jnp.uint32(16))
    logits = _dot(h2, wr_ref[...]) + br_ref[...]
    lane = lax.broadcasted_iota(jnp.int32, logits.shape, 1)
    vals, idxs = [], []
    work = logits
    for _ in range(TOP_K):
        m = jnp.max(work, axis=-1, keepdims=True)
        i = jnp.min(jnp.where(work == m, lane, LANES), axis=-1, keepdims=True)
        vals.append(m)
        idxs.append(i)
        work = jnp.where(lane == i, NEG_INF, work)
    exps = [jnp.exp(v - vals[0]) for v in vals]
    denom = exps[0] + exps[1] + exps[2] + exps[3]
    gate = jnp.zeros(logits.shape, F32)
    idx = jnp.zeros(logits.shape, jnp.int32)
    for k in range(TOP_K):
        gate = jnp.where(lane == k, exps[k] / denom, gate)
        idx = jnp.where(lane == k, idxs[k], idx)
    gate_ref[...] = gate
    idx_ref[...] = idx


def _out_router(x2d, o_att, o_ssm, w_out_b, norm_w, w_router_b, b_router_p):
    T, D = x2d.shape
    att_w = o_att.shape[1]
    ssm_w = o_ssm.shape[1]
    bm = _tile(T, 512)
    row = lambda i: (i, 0)
    const = lambda i: (0, 0)
    return pl.pallas_call(
        functools.partial(_out_router_kernel, att_w=att_w),
        grid=(T // bm,),
        in_specs=[
            pl.BlockSpec((bm, D), row),
            pl.BlockSpec((bm, att_w), row),
            pl.BlockSpec((bm, ssm_w), row),
            pl.BlockSpec((att_w + ssm_w, D), const),
            pl.BlockSpec((1, D), const),
            pl.BlockSpec((D, LANES), const),
            pl.BlockSpec((1, LANES), const),
        ],
        out_specs=[
            pl.BlockSpec((bm, D), row),
            pl.BlockSpec((bm, D // 2), row),
            pl.BlockSpec((bm, LANES), row),
            pl.BlockSpec((bm, LANES), row),
        ],
        out_shape=[
            jax.ShapeDtypeStruct((T, D), F32),
            jax.ShapeDtypeStruct((T, D // 2), jnp.uint32),
            jax.ShapeDtypeStruct((T, LANES), F32),
            jax.ShapeDtypeStruct((T, LANES), jnp.int32),
        ],
        compiler_params=_cparams(("parallel",)),
        name="out_proj_router",
    )(x2d, o_att, o_ssm, w_out_b, norm_w.reshape(1, D), w_router_b, b_router_p)


def _moe_kernel(be_ref, br_ref, nu_ref, x_ref, wg_ref, wu_ref, bg_ref, bu_ref, wd_ref, bd_ref, y_ref, xb_ref,
                *, sub):
    i = pl.program_id(0)
    f = pl.program_id(1)
    rows = br_ref[i]
    bm, half = x_ref.shape

    @pl.when((f == 0) & (rows > 0))
    def _():
        xp = x_ref[...]
        lo = lax.bitcast_convert_type(lax.shift_left(xp, jnp.uint32(16)), F32)
        hi = lax.bitcast_convert_type(xp & jnp.uint32(0xFFFF0000), F32)
        xb_ref[:, :half] = lo.astype(BF16)
        xb_ref[:, half:] = hi.astype(BF16)

    for r in range(bm // sub):
        rs = slice(r * sub, (r + 1) * sub)

        @pl.when(f == 0)
        def _():
            live = jnp.where(rows > r * sub, 1.0, 0.0)
            y_ref[rs, :] = jnp.broadcast_to(bd_ref[0] * live, (sub, y_ref.shape[1]))

        @pl.when(rows > r * sub)
        def _():
            x = xb_ref[rs, :]
            gate = jnp.minimum(_dot(x, wg_ref[0].astype(BF16)) + bg_ref[0], SWIGLU_LIMIT)
            up = jnp.clip(_dot(x, wu_ref[0].astype(BF16)) + bu_ref[0], -SWIGLU_LIMIT, SWIGLU_LIMIT)
            act = (gate * jax.nn.sigmoid(SWIGLU_ALPHA * gate) * (up + 1.0)).astype(BF16)
            y_ref[rs, :] += _dot(act, wd_ref[0].astype(BF16))


def _moe_ffn_blocks(x_sorted, block_e, block_rows, n_used, w_gate_up_l, b_gate_up_l, w_down_l, b_down_l, bm,
                    tf_pref=256, n_sub=2):
    R, half = x_sorted.shape
    D = 2 * half
    E, _, two_f = w_gate_up_l.shape
    d_ff = two_f // 2
    tf = _tile(d_ff, tf_pref)
    n_f = d_ff // tf
    n_blocks = R // bm
    sub = bm // n_sub
    bgu = b_gate_up_l.reshape(E, 1, two_f)
    bd = b_down_l.reshape(E, 1, D)

    def fe(i, f, br):
        return jnp.where(br[i] > 0, f, n_f - 1)

    def xi(i, br, nu):
        return jnp.where(br[i] > 0, i, jnp.maximum(nu[0] - 1, 0))

    grid_spec = pltpu.PrefetchScalarGridSpec(
        num_scalar_prefetch=3,
        grid=(n_blocks, n_f),
        in_specs=[
            pl.BlockSpec((bm, half), lambda i, f, be, br, nu: (xi(i, br, nu), 0)),
            pl.BlockSpec((1, D, tf), lambda i, f, be, br, nu: (be[i], 0, fe(i, f, br))),
            pl.BlockSpec((1, D, tf), lambda i, f, be, br, nu: (be[i], 0, n_f + fe(i, f, br))),
            pl.BlockSpec((1, 1, tf), lambda i, f, be, br, nu: (be[i], 0, fe(i, f, br))),
            pl.BlockSpec((1, 1, tf), lambda i, f, be, br, nu: (be[i], 0, n_f + fe(i, f, br))),
            pl.BlockSpec((1, tf, D), lambda i, f, be, br, nu: (be[i], fe(i, f, br), 0)),
            pl.BlockSpec((1, 1, D), lambda i, f, be, br, nu: (be[i], 0, 0)),
        ],
        out_specs=pl.BlockSpec((bm, D), lambda i, f, be, br, nu: (i, 0)),
        scratch_shapes=[pltpu.VMEM((bm, D), BF16)],
    )
    return pl.pallas_call(
        functools.partial(_moe_kernel, sub=sub),
        grid_spec=grid_spec,
        out_shape=jax.ShapeDtypeStruct((R, D), F32),
        compiler_params=_cparams(("parallel", "arbitrary")),
        name="moe_ffn",
    )(block_e, block_rows, n_used, x_sorted, w_gate_up_l, w_gate_up_l, bgu, bgu, w_down_l, bd)


def _moe_route(top_e, n_experts, bm):
    T, K = top_e.shape
    M = T * K
    n_blocks = (M + n_experts * (bm - 1)) // bm
    flat_e = top_e.reshape(-1)
    order = jnp.argsort(flat_e)
    rank = jnp.argsort(order)
    counts = jnp.bincount(flat_e, length=n_experts)
    padded = ((counts + bm - 1) // bm) * bm
    start = jnp.cumsum(counts) - counts
    pend = jnp.cumsum(padded)
    pstart = pend - padded
    pad_off = pstart - start
    slot_of = (rank + pad_off[flat_e]).astype(jnp.int32)
    slots = jnp.arange(n_blocks * bm)
    slot_e = jnp.minimum(jnp.searchsorted(pend, slots, side="right"), n_experts - 1)
    real = (slots - pstart[slot_e]) < counts[slot_e]
    src = jnp.clip(slots - pad_off[slot_e], 0, M - 1)
    slot_tok = jnp.where(real, order[src] // K, slots % T).astype(jnp.int32)
    n_used = pend[-1] // bm
    blk = jnp.arange(n_blocks)
    block_e = slot_e[::bm]
    block_rows = jnp.clip(counts[block_e] - (blk * bm - pstart[block_e]), 0, bm)
    block_rows = jnp.where(blk < n_used, block_rows, 0).astype(jnp.int32)
    last_e = block_e[jnp.maximum(n_used - 1, 0)]
    block_e = jnp.where(blk < n_used, block_e, last_e).astype(jnp.int32)
    return slot_tok, block_e, block_rows, n_used.astype(jnp.int32).reshape(1), slot_of.reshape(T, K)


def _combine_kernel(x_ref, *refs):
    yg_refs = refs[:TOP_K]
    gate_ref, nw_ref, xo_ref, yo_ref = refs[TOP_K:]
    gate = gate_ref[...]
    y = x_ref[...]
    for k in range(TOP_K):
        y = y + gate[:, k:k + 1] * yg_refs[k][0]
    xo_ref[...] = y
    ms = jnp.mean(y * y, axis=-1, keepdims=True)
    yo_ref[...] = (y * lax.rsqrt(ms + EPS)) * nw_ref[...]


def _combine(x1, yg, row0, gates, norm_w):
    T, D = x1.shape
    bm = _tile(math.gcd(T, row0) if row0 else T, 256)
    off = row0 // bm

    def yg_spec(k):
        return pl.BlockSpec((1, bm, D), lambda i: (k, off + i, 0))

    return pl.pallas_call(
        _combine_kernel,
        grid=(T // bm,),
        in_specs=[pl.BlockSpec((bm, D), lambda i: (i, 0))] + [yg_spec(k) for k in range(TOP_K)] + [
            pl.BlockSpec((bm, LANES), lambda i: (i, 0)),
            pl.BlockSpec((1, D), lambda i: (0, 0)),
        ],
        out_specs=[pl.BlockSpec((bm, D), lambda i: (i, 0)), pl.BlockSpec((bm, D), lambda i: (i, 0))],
        out_shape=[jax.ShapeDtypeStruct((T, D), F32), jax.ShapeDtypeStruct((T, D), F32)],
        compiler_params=_cparams(("parallel",)),
        name="moe_combine_norm",
    )(x1, *([yg] * TOP_K), gates, norm_w.reshape(1, D))


def kernel(x_prompt, x_sample, cache_k, cache_v, page_table, state_ssm, state_conv, meta_tokens,
           norm_mix_w, w_in, lambda_q1, lambda_k1, lambda_q2, lambda_k2, subln_w, conv_w, conv_b,
           dt_bias, a_log, d_skip, ssm_norm_w, w_out, norm_ffn_w, w_router, b_router,
           w_gate_up, b_gate_up, w_down, b_down, norm_final_w):
    depth = w_in.shape[0]
    Bp, Lp, D = x_prompt.shape
    Bs, Ss, _ = x_sample.shape
    n_heads_att = cache_v.shape[3]
    att_w = n_heads_att * ATT_V_DIM
    qk_w = n_heads_att * 2 * ATT_HEAD_DIM
    n_heads_ssm = state_ssm.shape[2]
    ssm_w = n_heads_ssm * SSM_HEAD_DIM
    conv_dim = state_conv.shape[-1]
    n_main = 2 * qk_w + att_w + ssm_w + conv_dim
    n_experts = w_router.shape[-1]
    n_pool, page = cache_k.shape[1], cache_k.shape[2]
    Tp, Ts = Bp * Lp, Bs * Ss
    Tsm = Ts + N_META
    chunk = 128

    xp = x_prompt.reshape(Tp, D)
    xsm = jnp.concatenate([x_sample.reshape(Ts, D), meta_tokens.astype(F32)], axis=0)

    outs = {k: [] for k in ("kp", "vp", "ks", "vs", "ssmp", "convp", "ssms", "convs")}
    for layer in range(depth):
        lam_init = _lambda_init(layer)
        lam_vec = jnp.stack([lambda_q1[layer], lambda_k1[layer], lambda_q2[layer], lambda_k2[layer]])
        w_main = w_in[layer][:, :n_main].astype(BF16)
        w_dt = jnp.pad(w_in[layer][:, n_main:], ((0, 0), (0, LANES - n_heads_ssm))).astype(BF16)
        pad_h = (0, LANES - n_heads_ssm)
        ssd_p = {
            "conv_w": conv_w[layer], "conv_b": conv_b[layer].reshape(1, conv_dim),
            "dt_bias": jnp.pad(dt_bias[layer], pad_h).reshape(1, LANES),
            "dt_bias_t": dt_bias[layer].reshape(n_heads_ssm, 1),
            "a_log": jnp.pad(a_log[layer], pad_h).reshape(1, LANES),
            "a_log_t": a_log[layer].reshape(n_heads_ssm, 1),
            "d_skip_x": jnp.repeat(d_skip[layer], SSM_HEAD_DIM).reshape(1, ssm_w),
            "norm_w": ssm_norm_w[layer].reshape(1, ssm_w),
            "expand": (jnp.arange(LANES)[:, None] == (jnp.arange(ssm_w) // SSM_HEAD_DIM)[None, :]).astype(BF16),
        }

        proj_p, dt_p = _rms_proj(xp, norm_mix_w[layer], w_main, w_dt)
        proj_sm, dt_sm = _rms_proj(xsm, norm_mix_w[layer], w_main, w_dt)
        proj_p3 = proj_p.reshape(Bp, Lp, n_main)
        proj_s3 = proj_sm[:Ts].reshape(Bs, Ss, n_main)
        proj_m = proj_sm[Ts:]

        def pad_rows(a, rows):
            return jnp.pad(a, ((0, 0), (0, rows - a.shape[1]), (0, 0)))

        proj_m_pad = pad_rows(proj_m[None], chunk)
        oa_m = _flash_attn(proj_m_pad, None, 0, lam_vec, subln_w[layer], lam_init, n_heads_att)
        oa_p = _flash_attn(proj_p3, proj_m_pad[0], N_META, lam_vec, subln_w[layer], lam_init, n_heads_att)
        ck = jnp.transpose(cache_k[layer], (0, 2, 3, 4, 1)).reshape(n_pool, qk_w, page)
        cv = cache_v[layer].reshape(n_pool, page * n_heads_att, ATT_V_DIM)
        oa_s = _sample_attn(proj_s3, ck, cv, page_table, lam_vec, subln_w[layer], lam_init, n_heads_att)

        zero_conv = jnp.zeros((1, CONV_PAD, conv_dim), F32)
        zero_state = jnp.zeros((1, SSM_GROUPS, SSM_STATE, ssm_w // SSM_GROUPS), F32)
        ys_m, conv_m, st_m = _ssd(proj_m_pad, pad_rows(dt_sm[Ts:][None], chunk),
                                  zero_conv, zero_state, True, N_META, ssd_p, n_heads_ssm)
        ys_p, conv_p, st_p = _ssd(proj_p3, dt_p.reshape(Bp, Lp, LANES), conv_m, st_m, True, chunk,
                                  ssd_p, n_heads_ssm)
        conv0_s = jnp.pad(state_conv[layer].astype(F32), ((0, 0), (CONV_PAD - (CONV_WIDTH - 1), 0), (0, 0)))
        ys_s, conv_s, st_s = _ssd(pad_rows(proj_s3, chunk), pad_rows(dt_sm[:Ts].reshape(Bs, Ss, LANES), chunk),
                                  conv0_s, _state_to_t(state_ssm[layer].astype(F32), n_heads_ssm),
                                  False, Ss, ssd_p, n_heads_ssm)

        w_out_b = w_out[layer].astype(BF16)
        w_router_b = jnp.pad(w_router[layer], ((0, 0), (0, LANES - n_experts))).astype(BF16)
        b_router_p = jnp.pad(b_router[layer].astype(F32), (0, LANES - n_experts),
                             constant_values=-1e30).reshape(1, LANES)
        oa_sm = jnp.concatenate([oa_s.reshape(Ts, att_w).astype(BF16), oa_m[0, :N_META]], axis=0)
        os_sm = jnp.concatenate([ys_s[:, :Ss].reshape(Ts, ssm_w), ys_m[0, :N_META]], axis=0)
        x1_p, h2_p, gate_p, idx_p = _out_router(xp, oa_p.reshape(Tp, att_w), ys_p.reshape(Tp, ssm_w),
                                                w_out_b, norm_ffn_w[layer], w_router_b, b_router_p)
        x1_sm, h2_sm, gate_sm, idx_sm = _out_router(xsm, oa_sm, os_sm, w_out_b, norm_ffn_w[layer],
                                                    w_router_b, b_router_p)

        h2 = jnp.concatenate([h2_p, h2_sm], axis=0)
        top_e = jnp.concatenate([idx_p[:, :TOP_K], idx_sm[:, :TOP_K]], axis=0)
        T_all = Tp + Tsm
        bm_moe = 1024 if T_all * TOP_K >= 1024 * n_experts else 256
        slot_tok, block_e, block_rows, n_used, slot_of = _moe_route(top_e, n_experts, bm_moe)
        x_sorted = h2[slot_tok]
        yb = _moe_ffn_blocks(x_sorted, block_e, block_rows, n_used, w_gate_up[layer], b_gate_up[layer],
                             w_down[layer], b_down[layer], bm_moe)
        yg = yb[slot_of.T.reshape(-1)].reshape(TOP_K, T_all, D)
        norm_last = norm_final_w if layer == depth - 1 else jnp.ones((D,), F32)
        xp, y_p = _combine(x1_p, yg, 0, gate_p, norm_last)
        xsm, y_sm = _combine(x1_sm, yg, Tp, gate_sm, norm_last)

        k_m = jnp.broadcast_to(proj_m[None, :, qk_w:2 * qk_w], (Bp, N_META, qk_w))
        v_m = jnp.broadcast_to(proj_m[None, :, 2 * qk_w:2 * qk_w + att_w], (Bp, N_META, att_w))
        k_p = jnp.concatenate([k_m, proj_p3[:, :, qk_w:2 * qk_w]], axis=1)
        v_p = jnp.concatenate([v_m, proj_p3[:, :, 2 * qk_w:2 * qk_w + att_w]], axis=1)
        outs["kp"].append(k_p.reshape(Bp, N_META + Lp, n_heads_att, 2, ATT_HEAD_DIM))
        outs["vp"].append(v_p.reshape(Bp, N_META + Lp, n_heads_att, ATT_V_DIM))
        outs["ks"].append(proj_s3[:, :, qk_w:2 * qk_w].reshape(Bs, Ss, n_heads_att, 2, ATT_HEAD_DIM))
        outs["vs"].append(proj_s3[:, :, 2 * qk_w:2 * qk_w + att_w].reshape(Bs, Ss, n_heads_att, ATT_V_DIM))
        outs["ssmp"].append(_state_from_t(st_p, n_heads_ssm))
        outs["convp"].append(conv_p[:, CONV_PAD - (CONV_WIDTH - 1):])
        outs["ssms"].append(_state_from_t(st_s, n_heads_ssm))
        outs["convs"].append(conv_s[:, CONV_PAD - (CONV_WIDTH - 1):])

    y_prompt = y_p.reshape(Bp, Lp, D)
    y_sample = y_sm[:Ts].reshape(Bs, Ss, D)
    return (y_prompt, y_sample, jnp.stack(outs["kp"]), jnp.stack(outs["vp"]), jnp.stack(outs["ks"]),
            jnp.stack(outs["vs"]), jnp.stack(outs["ssmp"]), jnp.stack(outs["convp"]),
            jnp.stack(outs["ssms"]), jnp.stack(outs["convs"]))
```

```python
import functools
import math

import jax
import jax.numpy as jnp
from jax import lax
from jax.experimental import pallas as pl
from jax.experimental.pallas import tpu as pltpu

F32 = jnp.float32
BF16 = jnp.bfloat16

N_META = 16
ATT_HEAD_DIM = 64
ATT_V_DIM = 128
SSM_HEAD_DIM = 64
SSM_GROUPS = 2
SSM_STATE = 128
CONV_WIDTH = 4
TOP_K = 4
SWIGLU_ALPHA = 1.702
SWIGLU_LIMIT = 7.0
EPS = 1e-5
ATT_SCALE = ATT_HEAD_DIM ** -0.5

LANES = 128
SUBLANES = 8
VMEM_LIMIT_BYTES = 56 * 1024 * 1024
NEG_INF = float("-inf")
CONV_PAD = SUBLANES


def _tile(n, pref):
    if n <= pref:
        return n
    for t in range(pref, 0, -1):
        if n % t == 0 and t % SUBLANES == 0:
            return t
    return n


def _cparams(sem):
    return pltpu.CompilerParams(dimension_semantics=sem, vmem_limit_bytes=VMEM_LIMIT_BYTES)


def _lambda_init(layer):
    return 0.8 - 0.6 * math.exp(-0.3 * layer)


def _lam_from_ref(lam_ref, lam_init):
    lv = lam_ref[...]
    s1 = jnp.sum(lv[0:1] * lv[1:2], axis=-1, keepdims=True)
    s2 = jnp.sum(lv[2:3] * lv[3:4], axis=-1, keepdims=True)
    return jnp.exp(s1) - jnp.exp(s2) + lam_init


def _split3(x):
    hi = x.astype(BF16)
    r1 = x - hi.astype(F32)
    mid = r1.astype(BF16)
    lo = (r1 - mid.astype(F32)).astype(BF16)
    return hi, mid, lo


def _dot(a, b):
    return jnp.dot(a, b, preferred_element_type=F32)


def _dot_nt(a, b):
    return lax.dot_general(a, b, (((1,), (1,)), ((), ())), preferred_element_type=F32)


def _rms_proj_kernel(x_ref, nw_ref, w_ref, wdt_ref, proj_ref, dt_ref, h_ref):
    @pl.when(pl.program_id(1) == 0)
    def _():
        x = x_ref[...]
        ms = jnp.mean(x * x, axis=-1, keepdims=True)
        h = ((x * lax.rsqrt(ms + EPS)) * nw_ref[...]).astype(BF16)
        h_ref[...] = h
        dt_ref[...] = _dot(h, wdt_ref[...])

    proj_ref[...] = _dot(h_ref[...], w_ref[...])


def _rms_proj(x2d, norm_w, w_main, w_dt):
    T, D = x2d.shape
    N = w_main.shape[1]
    bm = _tile(T, 1024)
    bn = _tile(N, 512)
    return pl.pallas_call(
        _rms_proj_kernel,
        grid=(T // bm, N // bn),
        in_specs=[
            pl.BlockSpec((bm, D), lambda i, j: (i, 0)),
            pl.BlockSpec((1, D), lambda i, j: (0, 0)),
            pl.BlockSpec((D, bn), lambda i, j: (0, j)),
            pl.BlockSpec((D, LANES), lambda i, j: (0, 0)),
        ],
        out_specs=[
            pl.BlockSpec((bm, bn), lambda i, j: (i, j)),
            pl.BlockSpec((bm, LANES), lambda i, j: (i, 0)),
        ],
        out_shape=[jax.ShapeDtypeStruct((T, N), F32), jax.ShapeDtypeStruct((T, LANES), F32)],
        scratch_shapes=[pltpu.VMEM((bm, D), BF16)],
        compiler_params=_cparams(("parallel", "arbitrary")),
        name="rms_proj",
    )(x2d, norm_w.reshape(1, D), w_main, w_dt)


def _flash_kernel(*refs, tq, n_prefix, lam_init):
    if n_prefix:
        (qi_tab, ki_tab, lam_ref, sub_ref, q_ref, k_ref, v_ref, pk_ref, pv_ref,
         o_ref, m_ref, l_ref, acc_ref) = refs
    else:
        (qi_tab, ki_tab, lam_ref, sub_ref, q_ref, k_ref, v_ref,
         o_ref, m_ref, l_ref, acc_ref) = refs
        pk_ref = pv_ref = None
    t = pl.program_id(2)
    _flash_body(lam_ref, sub_ref, q_ref, k_ref, v_ref, pk_ref, pv_ref, o_ref, m_ref, l_ref, acc_ref,
                qi_tab[t], ki_tab[t], tq=tq, n_prefix=n_prefix, lam_init=lam_init)


def _flash_body(lam_ref, sub_ref, q_ref, k_ref, v_ref, pk_ref, pv_ref, o_ref, m_ref, l_ref, acc_ref,
                qi, ki, *, tq, n_prefix, lam_init):
    q = q_ref[0] * ATT_SCALE
    lane = lax.broadcasted_iota(jnp.int32, q.shape, 1)
    q_maps = (jnp.where(lane < ATT_HEAD_DIM, q, 0.0).astype(BF16),
              jnp.where(lane >= ATT_HEAD_DIM, q, 0.0).astype(BF16))

    def update(k, v, mask):
        kb = k.astype(BF16)
        vb = v.astype(BF16)
        reps = k.shape[0] // LANES
        for c in range(2):
            s = _dot_nt(q_maps[c], kb)
            if mask is not None:
                s = jnp.where(mask, s, NEG_INF)
            m_prev = m_ref[c]
            m_new = jnp.maximum(m_prev, jnp.max(s, axis=-1, keepdims=True))
            corr = jnp.exp(m_prev - m_new)
            p = jnp.exp(s - jnp.concatenate([m_new] * reps, axis=1))
            part = p[:, 0:LANES]
            for j in range(1, reps):
                part = part + p[:, j * LANES:(j + 1) * LANES]
            l_ref[c] = l_ref[c] * corr + part
            acc_ref[c] = acc_ref[c] * corr + _dot(p.astype(BF16), vb)
            m_ref[c] = m_new

    @pl.when(ki == 0)
    def _():
        m_ref[...] = jnp.full(m_ref.shape, NEG_INF, F32)
        l_ref[...] = jnp.zeros(l_ref.shape, F32)
        acc_ref[...] = jnp.zeros(acc_ref.shape, F32)
        if n_prefix:
            col = lax.broadcasted_iota(jnp.int32, (tq, pk_ref.shape[0]), 1)
            update(pk_ref[...], pv_ref[...], col < n_prefix)

    @pl.when(ki < qi)
    def _():
        update(k_ref[0], v_ref[0], None)

    @pl.when(ki == qi)
    def _():
        row = lax.broadcasted_iota(jnp.int32, (tq, tq), 0)
        col = lax.broadcasted_iota(jnp.int32, (tq, tq), 1)
        update(k_ref[0], v_ref[0], row >= col)
        lam = _lam_from_ref(lam_ref, lam_init)
        l0 = jnp.sum(l_ref[0], axis=-1, keepdims=True)
        l1 = jnp.sum(l_ref[1], axis=-1, keepdims=True)
        o = acc_ref[0] / l0 - lam * (acc_ref[1] / l1)
        ms = jnp.mean(o * o, axis=-1, keepdims=True)
        y = (o * lax.rsqrt(ms + EPS)) * sub_ref[...]
        o_ref[0] = (y * (1.0 - lam_init)).astype(o_ref.dtype)


def _flash_attn(proj3, prefix, n_prefix, lam_vec, subln_w, lam_init, n_heads, out_dtype=BF16):
    B, L, _ = proj3.shape
    assert L % LANES == 0
    tq = _tile(L, 512)
    nq = L // tq
    pairs = [(a, b) for a in range(nq) for b in range(a + 1)]
    qi_tab = jnp.asarray([p[0] for p in pairs], jnp.int32)
    ki_tab = jnp.asarray([p[1] for p in pairs], jnp.int32)
    H = n_heads
    has_prefix = prefix is not None
    if not has_prefix:
        n_prefix = 0
    in_specs = [
        pl.BlockSpec((4, ATT_HEAD_DIM), lambda b, h, t, qt, kt: (0, 0)),
        pl.BlockSpec((1, ATT_V_DIM), lambda b, h, t, qt, kt: (0, 0)),
        pl.BlockSpec((1, tq, LANES), lambda b, h, t, qt, kt: (b, qt[t], h)),
        pl.BlockSpec((1, tq, LANES), lambda b, h, t, qt, kt: (b, kt[t], H + h)),
        pl.BlockSpec((1, tq, LANES), lambda b, h, t, qt, kt: (b, kt[t], 2 * H + h)),
    ]
    args = [lam_vec, subln_w.reshape(1, ATT_V_DIM), proj3, proj3, proj3]
    if has_prefix:
        P = prefix.shape[0]
        in_specs += [
            pl.BlockSpec((P, LANES), lambda b, h, t, qt, kt: (0, H + h)),
            pl.BlockSpec((P, LANES), lambda b, h, t, qt, kt: (0, 2 * H + h)),
        ]
        args += [prefix, prefix]
    grid_spec = pltpu.PrefetchScalarGridSpec(
        num_scalar_prefetch=2,
        grid=(B, H, len(pairs)),
        in_specs=in_specs,
        out_specs=pl.BlockSpec((1, tq, LANES), lambda b, h, t, qt, kt: (b, qt[t], h)),
        scratch_shapes=[
            pltpu.VMEM((2, tq, LANES), F32),
            pltpu.VMEM((2, tq, LANES), F32),
            pltpu.VMEM((2, tq, ATT_V_DIM), F32),
        ],
    )
    return pl.pallas_call(
        functools.partial(_flash_kernel, tq=tq, n_prefix=n_prefix, lam_init=lam_init),
        grid_spec=grid_spec,
        out_shape=jax.ShapeDtypeStruct((B, L, H * ATT_V_DIM), out_dtype),
        compiler_params=_cparams(("parallel", "parallel", "arbitrary")),
        name="flash_diff_attn",
    )(qi_tab, ki_tab, *args)


def _sample_body(lam_ref, sub_ref, q_ref, kn_ref, vn_ref, k_refs, v_refs, o_ref, qbd_ref, m_ref, l_ref, acc_ref,
                 step, n_steps, *, n_heads, s_new, page, lam_init):
    n_rows = n_heads * 2 * s_new
    width = n_heads * 2 * ATT_HEAD_DIM

    def update(ss, v_head):
        m_prev = m_ref[...]
        m_new = m_prev
        for s in ss:
            m_new = jnp.maximum(m_new, jnp.max(s, axis=-1, keepdims=True))
        corr = jnp.exp(m_prev - m_new)
        ps = [jnp.exp(s - m_new) for s in ss]
        l_new = l_ref[...] * corr
        for p in ps:
            l_new = l_new + jnp.sum(p, axis=-1, keepdims=True)
        l_ref[...] = l_new
        m_ref[...] = m_new
        pbs = [p.astype(BF16) for p in ps]
        for h in range(n_heads):
            r0, r1 = h * 2 * s_new, (h + 1) * 2 * s_new
            pv = None
            for j, pb in enumerate(pbs):
                d = _dot(pb[r0:r1, :], v_head(j, h))
                pv = d if pv is None else pv + d
            acc_ref[r0:r1, :] = acc_ref[r0:r1, :] * corr[r0:r1] + pv

    @pl.when(step == 0)
    def _():
        q = q_ref[0] * ATT_SCALE
        qt = jnp.concatenate([q] * (2 * n_heads), axis=0)
        row_hc = lax.broadcasted_iota(jnp.int32, (n_rows, width), 0) // s_new
        col_hc = lax.broadcasted_iota(jnp.int32, (n_rows, width), 1) // ATT_HEAD_DIM
        qbd = jnp.where(row_hc == col_hc, qt, 0.0).astype(BF16)
        qbd_ref[...] = qbd
        m_ref[...] = jnp.full(m_ref.shape, NEG_INF, F32)
        l_ref[...] = jnp.zeros(l_ref.shape, F32)
        acc_ref[...] = jnp.zeros(acc_ref.shape, F32)
        pad = jnp.zeros((page - s_new, width), F32)
        k_self = jnp.concatenate([kn_ref[0], pad], axis=0).astype(BF16)
        v_self = jnp.concatenate([vn_ref[0], pad], axis=0).astype(BF16)
        r = lax.broadcasted_iota(jnp.int32, (n_rows, page), 0)
        c = lax.broadcasted_iota(jnp.int32, (n_rows, page), 1)
        s_self = jnp.where(c <= (r % s_new), _dot_nt(qbd, k_self), NEG_INF)
        update([s_self], lambda j, h: v_self[:, h * ATT_V_DIM:(h + 1) * ATT_V_DIM])

    qbd = qbd_ref[...]
    update([_dot(qbd, k[0].astype(BF16)) for k in k_refs],
           lambda j, h: v_refs[j][0, pl.ds(h, page, stride=n_heads), :].astype(BF16))

    @pl.when(step == n_steps - 1)
    def _():
        lam = _lam_from_ref(lam_ref, lam_init)
        o_all = acc_ref[...] / l_ref[...]
        for h in range(n_heads):
            r0 = h * 2 * s_new
            o = o_all[r0:r0 + s_new] - lam * o_all[r0 + s_new:r0 + 2 * s_new]
            ms = jnp.mean(o * o, axis=-1, keepdims=True)
            y = (o * lax.rsqrt(ms + EPS)) * sub_ref[...]
            o_ref[0, :, h * ATT_V_DIM:(h + 1) * ATT_V_DIM] = y * (1.0 - lam_init)


def _attn_fused_kernel(*refs, pps, spb, n_s, tq, n_prefix, n_heads, s_new, page, lam_init):
    b_tab, h_tab, qi_tab, ki_tab, pt_ref = refs[:5]
    lam_ref, sub_ref, q_ref, k_ref, v_ref, pk_ref, pv_ref, qn_ref, kn_ref, vn_ref = refs[5:15]
    k_refs = refs[15:15 + pps]
    v_refs = refs[15 + pps:15 + 2 * pps]
    o_ref, os_ref, m_ref, l_ref, acc_ref, qbd_ref, ms_ref, ls_ref, accs_ref = refs[15 + 2 * pps:]
    g = pl.program_id(0)
    _flash_body(lam_ref, sub_ref, q_ref, k_ref, v_ref, pk_ref, pv_ref, o_ref, m_ref, l_ref, acc_ref,
                qi_tab[g], ki_tab[g], tq=tq, n_prefix=n_prefix, lam_init=lam_init)

    @pl.when(g < n_s)
    def _():
        _sample_body(lam_ref, sub_ref, qn_ref, kn_ref, vn_ref, k_refs, v_refs, os_ref, qbd_ref, ms_ref,
                     ls_ref, accs_ref, g % spb, spb, n_heads=n_heads, s_new=s_new, page=page,
                     lam_init=lam_init)


def _attn_fused(proj3, prefix, n_prefix, proj_s3, cache_k_l, cache_v_l, page_table, lam_vec, subln_w,
                lam_init, n_heads):
    B, L, _ = proj3.shape
    assert L % LANES == 0
    tq = _tile(L, 512)
    nq = L // tq
    H = n_heads
    steps = [(b, h, a, c) for b in range(B) for h in range(H) for a in range(nq) for c in range(a + 1)]
    G = len(steps)
    tabs = [jnp.asarray([s[i] for s in steps], jnp.int32) for i in range(4)]
    bs, S, _ = proj_s3.shape
    n_pool, width, page = cache_k_l.shape
    n_pages = page_table.shape[1]
    assert bs <= G
    pps = next(p for p in range(1, n_pages + 1) if n_pages % p == 0 and bs * (n_pages // p) <= G)
    spb = n_pages // pps
    n_s = bs * spb
    n_rows = n_heads * 2 * S
    P = prefix.shape[0]

    def sb(g):
        return jnp.minimum(g, n_s - 1) // spb

    def pg(g):
        return jnp.minimum(g, n_s - 1) % spb

    def k_spec(j):
        return pl.BlockSpec((1, width, page), lambda g, bt, ht, qt, kt, pt: (pt[sb(g), pg(g) * pps + j], 0, 0))

    def v_spec(j):
        return pl.BlockSpec((1, page * n_heads, ATT_V_DIM),
                            lambda g, bt, ht, qt, kt, pt: (pt[sb(g), pg(g) * pps + j], 0, 0))

    in_specs = [
        pl.BlockSpec((4, ATT_HEAD_DIM), lambda g, bt, ht, qt, kt, pt: (0, 0)),
        pl.BlockSpec((1, ATT_V_DIM), lambda g, bt, ht, qt, kt, pt: (0, 0)),
        pl.BlockSpec((1, tq, LANES), lambda g, bt, ht, qt, kt, pt: (bt[g], qt[g], ht[g])),
        pl.BlockSpec((1, tq, LANES), lambda g, bt, ht, qt, kt, pt: (bt[g], kt[g], H + ht[g])),
        pl.BlockSpec((1, tq, LANES), lambda g, bt, ht, qt, kt, pt: (bt[g], kt[g], 2 * H + ht[g])),
        pl.BlockSpec((P, LANES), lambda g, bt, ht, qt, kt, pt: (0, H + ht[g])),
        pl.BlockSpec((P, LANES), lambda g, bt, ht, qt, kt, pt: (0, 2 * H + ht[g])),
        pl.BlockSpec((1, S, width), lambda g, bt, ht, qt, kt, pt: (sb(g), 0, 0)),
        pl.BlockSpec((1, S, width), lambda g, bt, ht, qt, kt, pt: (sb(g), 0, 1)),
        pl.BlockSpec((1, S, width), lambda g, bt, ht, qt, kt, pt: (sb(g), 0, 2)),
    ] + [k_spec(j) for j in range(pps)] + [v_spec(j) for j in range(pps)]
    grid_spec = pltpu.PrefetchScalarGridSpec(
        num_scalar_prefetch=5,
        grid=(G,),
        in_specs=in_specs,
        out_specs=[
            pl.BlockSpec((1, tq, LANES), lambda g, bt, ht, qt, kt, pt: (bt[g], qt[g], ht[g])),
            pl.BlockSpec((1, S, width), lambda g, bt, ht, qt, kt, pt: (sb(g), 0, 0)),
        ],
        scratch_shapes=[
            pltpu.VMEM((2, tq, LANES), F32),
            pltpu.VMEM((2, tq, LANES), F32),
            pltpu.VMEM((2, tq, ATT_V_DIM), F32),
            pltpu.VMEM((n_rows, width), BF16),
            pltpu.VMEM((n_rows, 1), F32),
            pltpu.VMEM((n_rows, 1), F32),
            pltpu.VMEM((n_rows, ATT_V_DIM), F32),
        ],
    )
    return pl.pallas_call(
        functools.partial(_attn_fused_kernel, pps=pps, spb=spb, n_s=n_s, tq=tq, n_prefix=n_prefix,
                          n_heads=n_heads, s_new=S, page=page, lam_init=lam_init),
        grid_spec=grid_spec,
        out_shape=[jax.ShapeDtypeStruct((B, L, H * ATT_V_DIM), BF16),
                   jax.ShapeDtypeStruct((bs, S, width), F32)],
        compiler_params=_cparams(("arbitrary",)),
        name="attn_fused",
    )(*tabs, page_table, lam_vec, subln_w.reshape(1, ATT_V_DIM), proj3, proj3, proj3, prefix, prefix,
      proj_s3, proj_s3, proj_s3, *([cache_k_l] * pps), *([cache_v_l] * pps))


def _ssd_kernel(z_ref, xa_ref, xb_ref, bc_ref, dt_ref, dtt_ref, conv0_ref, s0_ref,
                cw_ref, cb_ref, dtb_ref, dtbt_ref, alog_ref, alogt_ref, dsk_ref, nw_ref, ex_ref,
                y_ref, convo_ref, so_ref, buf_ref, st_ref, *, Q, valid, n_heads):
    c = pl.program_id(1)
    n_c = pl.num_programs(1)
    ssm_w = n_heads * SSM_HEAD_DIM
    gw = ssm_w // SSM_GROUPS
    hg = n_heads // SSM_GROUPS

    @pl.when(c == 0)
    def _():
        buf_ref[0:CONV_PAD, :] = conv0_ref[0]
        st_ref[...] = s0_ref[0]

    buf_ref[CONV_PAD:CONV_PAD + Q, 0:gw] = xa_ref[0]
    buf_ref[CONV_PAD:CONV_PAD + Q, gw:2 * gw] = xb_ref[0]
    buf_ref[CONV_PAD:CONV_PAD + Q, 2 * gw:3 * gw] = bc_ref[0]
    base = CONV_PAD - (CONV_WIDTH - 1)
    conv = cb_ref[...]
    for w in range(CONV_WIDTH):
        conv = conv + buf_ref[base + w:base + w + Q, :] * cw_ref[w:w + 1, :]
    xbc = conv * jax.nn.sigmoid(conv)
    xs = xbc[:, :ssm_w]
    bmat = xbc[:, ssm_w:ssm_w + SSM_GROUPS * SSM_STATE]
    cmat = xbc[:, ssm_w + SSM_GROUPS * SSM_STATE:]

    lane = lax.broadcasted_iota(jnp.int32, (1, LANES), 1)
    a_row = jnp.where(lane < n_heads, -jnp.exp(alog_ref[...]), 0.0)
    dt = jax.nn.softplus(dt_ref[0] + dtb_ref[...])
    a_col = -jnp.exp(alogt_ref[...])
    dtt = jax.nn.softplus(dtt_ref[0] + dtbt_ref[...])
    if valid < Q:
        dt = jnp.where(lax.broadcasted_iota(jnp.int32, dt.shape, 0) < valid, dt, 0.0)
        dtt = jnp.where(lax.broadcasted_iota(jnp.int32, dtt.shape, 1) < valid, dtt, 0.0)
    da = dt * a_row
    dat = dtt * a_col

    r_i = lax.broadcasted_iota(jnp.int32, (Q, Q), 0)
    c_i = lax.broadcasted_iota(jnp.int32, (Q, Q), 1)
    causal = r_i >= c_i
    tri_l = jnp.where(causal, 1.0, 0.0).astype(BF16)
    tri_u = jnp.where(r_i <= c_i, 1.0, 0.0).astype(BF16)
    da3 = _split3(da)
    a_cum = _dot(tri_l, da3[0]) + _dot(tri_l, da3[1]) + _dot(tri_l, da3[2])
    dat3 = _split3(dat)
    a_cum_t = _dot(dat3[0], tri_u) + _dot(dat3[1], tri_u) + _dot(dat3[2], tri_u)

    ex = ex_ref[...]
    dt3 = _split3(dt)
    dt_x = _dot(dt3[0], ex) + _dot(dt3[1], ex) + _dot(dt3[2], ex)
    ac3 = _split3(a_cum)
    acum_x = _dot(ac3[0], ex) + _dot(ac3[1], ex) + _dot(ac3[2], ex)
    a_last_x = acum_x[Q - 1:Q, :]
    e_in = jnp.exp(acum_x)
    dec_end = jnp.exp(a_last_x - acum_x)
    chunk_decay = jnp.exp(a_last_x)

    xdt = xs * dt_x
    xdt_b = xdt.astype(BF16)
    xdec_b = (xdt * dec_end).astype(BF16)
    lane_q = lax.broadcasted_iota(jnp.int32, (Q, LANES), 1)

    y_groups = []
    for g in range(SSM_GROUPS):
        cg = cmat[:, g * SSM_STATE:(g + 1) * SSM_STATE].astype(BF16)
        bg = bmat[:, g * SSM_STATE:(g + 1) * SSM_STATE]
        gmat = _dot_nt(cg, bg.astype(BF16))
        st_old = st_ref[g]
        y_off = _dot(cg, st_old.astype(BF16)) * e_in[:, g * gw:(g + 1) * gw]
        pieces = []
        for j in range(hg // 2):
            halves = []
            for u in range(2):
                hh = g * hg + 2 * j + u
                seg = jnp.exp(jnp.where(causal, a_cum[:, hh:hh + 1] - a_cum_t[hh:hh + 1, :], NEG_INF))
                mh = (gmat * seg).astype(BF16)
                blk = (g * hg + 2 * j) * SSM_HEAD_DIM
                halves.append(_dot(mh, xdt_b[:, blk:blk + LANES]))
            pieces.append(jnp.where(lane_q < SSM_HEAD_DIM, halves[0], halves[1]))
        y_diag = jnp.concatenate(pieces, axis=1)
        y_groups.append(y_diag + y_off)
        st_ref[g] = (st_old * chunk_decay[:, g * gw:(g + 1) * gw]
                     + _dot(bg.T.astype(BF16), xdec_b[:, g * gw:(g + 1) * gw]))
    y = jnp.concatenate(y_groups, axis=1) + dsk_ref[...] * xs

    zv = z_ref[0]
    y = y * (zv * jax.nn.sigmoid(zv))
    outs = []
    for g in range(SSM_GROUPS):
        yg = y[:, g * gw:(g + 1) * gw]
        ms = jnp.mean(yg * yg, axis=-1, keepdims=True)
        outs.append((yg * lax.rsqrt(ms + EPS)) * nw_ref[:, g * gw:(g + 1) * gw])
    y_ref[0] = jnp.concatenate(outs, axis=1).astype(y_ref.dtype)

    tail = buf_ref[valid:valid + CONV_PAD, :]
    buf_ref[0:CONV_PAD, :] = tail

    @pl.when(c == n_c - 1)
    def _():
        convo_ref[0] = tail
        so_ref[0] = st_ref[...]


def _ssd(proj3, dt3, conv0, s0t, shared_init, valid, p, n_heads):
    B, L, _ = proj3.shape
    ssm_w = n_heads * SSM_HEAD_DIM
    gw = ssm_w // SSM_GROUPS
    conv_dim = ssm_w + 2 * SSM_GROUPS * SSM_STATE
    Q = min(L, 128)
    n_c = L // Q
    if valid < Q:
        assert n_c == 1
    else:
        valid = Q
    dtt3 = jnp.swapaxes(dt3[:, :, :n_heads], 1, 2)
    z_blk = 3 * ssm_w // ssm_w
    x_blk = 4 * ssm_w // gw
    init_idx = (lambda b, c: (0, 0, 0)) if shared_init else (lambda b, c: (b, 0, 0))
    init_idx4 = (lambda b, c: (0, 0, 0, 0)) if shared_init else (lambda b, c: (b, 0, 0, 0))
    const2 = lambda b, c: (0, 0)
    in_specs = [
        pl.BlockSpec((1, Q, ssm_w), lambda b, c: (b, c, z_blk)),
        pl.BlockSpec((1, Q, gw), lambda b, c: (b, c, x_blk)),
        pl.BlockSpec((1, Q, gw), lambda b, c: (b, c, x_blk + 1)),
        pl.BlockSpec((1, Q, gw), lambda b, c: (b, c, x_blk + 2)),
        pl.BlockSpec((1, Q, LANES), lambda b, c: (b, c, 0)),
        pl.BlockSpec((1, n_heads, Q), lambda b, c: (b, 0, c)),
        pl.BlockSpec((1, CONV_PAD, conv_dim), init_idx),
        pl.BlockSpec((1, SSM_GROUPS, SSM_STATE, gw), init_idx4),
        pl.BlockSpec((CONV_WIDTH, conv_dim), const2),
        pl.BlockSpec((1, conv_dim), const2),
        pl.BlockSpec((1, LANES), const2),
        pl.BlockSpec((n_heads, 1), const2),
        pl.BlockSpec((1, LANES), const2),
        pl.BlockSpec((n_heads, 1), const2),
        pl.BlockSpec((1, ssm_w), const2),
        pl.BlockSpec((1, ssm_w), const2),
        pl.BlockSpec((LANES, ssm_w), const2),
    ]
    out_specs = [
        pl.BlockSpec((1, Q, ssm_w), lambda b, c: (b, c, 0)),
        pl.BlockSpec((1, CONV_PAD, conv_dim), lambda b, c: (b, 0, 0)),
        pl.BlockSpec((1, SSM_GROUPS, SSM_STATE, gw), lambda b, c: (b, 0, 0, 0)),
    ]
    out_shape = [
        jax.ShapeDtypeStruct((B, L, ssm_w), BF16),
        jax.ShapeDtypeStruct((B, CONV_PAD, conv_dim), F32),
        jax.ShapeDtypeStruct((B, SSM_GROUPS, SSM_STATE, gw), F32),
    ]
    return pl.pallas_call(
        functools.partial(_ssd_kernel, Q=Q, valid=valid, n_heads=n_heads),
        grid=(B, n_c),
        in_specs=in_specs,
        out_specs=out_specs,
        out_shape=out_shape,
        scratch_shapes=[
            pltpu.VMEM((Q + CONV_PAD, conv_dim), F32),
            pltpu.VMEM((SSM_GROUPS, SSM_STATE, gw), F32),
        ],
        compiler_params=_cparams(("parallel", "arbitrary")),
        name="conv_ssd",
    )(proj3, proj3, proj3, proj3, dt3, dtt3, conv0, s0t,
      p["conv_w"], p["conv_b"], p["dt_bias"], p["dt_bias_t"], p["a_log"], p["a_log_t"],
      p["d_skip_x"], p["norm_w"], p["expand"])


def _state_to_t(s, n_heads):
    B = s.shape[0]
    hg = n_heads // SSM_GROUPS
    s = s.reshape(B, SSM_GROUPS, hg * SSM_HEAD_DIM, SSM_STATE)
    return jnp.swapaxes(s, 2, 3)


def _state_from_t(st, n_heads):
    B = st.shape[0]
    return jnp.swapaxes(st, 2, 3).reshape(B, n_heads, SSM_HEAD_DIM, SSM_STATE)


def _out_router_kernel(x_ref, oa_ref, os_ref, wo_ref, nw_ref, wr_ref, br_ref,
                       x1_ref, h2_ref, gate_ref, idx_ref, *, att_w):
    x1 = x_ref[...] + _dot(oa_ref[...], wo_ref[0:att_w, :]) + _dot(os_ref[...], wo_ref[att_w:, :])
    x1_ref[...] = x1
    ms = jnp.mean(x1 * x1, axis=-1, keepdims=True)
    h2 = ((x1 * lax.rsqrt(ms + EPS)) * nw_ref[...]).astype(BF16)
    half = h2.shape[1] // 2
    h2f = h2.astype(F32)
    lo_bits = lax.bitcast_convert_type(h2f[:, :half], jnp.uint32)
    hi_bits = lax.bitcast_convert_type(h2f[:, half:], jnp.uint32)
    h2_ref[...] = hi_bits | lax.shift_right_logical(lo_bits, jnp.uint32(16))
    logits = _dot(h2, wr_ref[...]) + br_ref[...]
    lane = lax.broadcasted_iota(jnp.int32, logits.shape, 1)
    vals, idxs = [], []
    work = logits
    for _ in range(TOP_K):
        m = jnp.max(work, axis=-1, keepdims=True)
        i = jnp.min(jnp.where(work == m, lane, LANES), axis=-1, keepdims=True)
        vals.append(m)
        idxs.append(i)
        work = jnp.where(lane == i, NEG_INF, work)
    exps = [jnp.exp(v - vals[0]) for v in vals]
    denom = exps[0] + exps[1] + exps[2] + exps[3]
    gate = jnp.zeros(logits.shape, F32)
    idx = jnp.zeros(logits.shape, jnp.int32)
    for k in range(TOP_K):
        gate = jnp.where(lane == k, exps[k] / denom, gate)
        idx = jnp.where(lane == k, idxs[k], idx)
    gate_ref[...] = gate
    idx_ref[...] = idx


def _out_router(x2d, o_att, o_ssm, w_out_b, norm_w, w_router_b, b_router_p):
    T, D = x2d.shape
    att_w = o_att.shape[1]
    ssm_w = o_ssm.shape[1]
    bm = _tile(T, 512)
    row = lambda i: (i, 0)
    const = lambda i: (0, 0)
    return pl.pallas_call(
        functools.partial(_out_router_kernel, att_w=att_w),
        grid=(T // bm,),
        in_specs=[
            pl.BlockSpec((bm, D), row),
            pl.BlockSpec((bm, att_w), row),
            pl.BlockSpec((bm, ssm_w), row),
            pl.BlockSpec((att_w + ssm_w, D), const),
            pl.BlockSpec((1, D), const),
            pl.BlockSpec((D, LANES), const),
            pl.BlockSpec((1, LANES), const),
        ],
        out_specs=[
            pl.BlockSpec((bm, D), row),
            pl.BlockSpec((bm, D // 2), row),
            pl.BlockSpec((bm, LANES), row),
            pl.BlockSpec((bm, LANES), row),
        ],
        out_shape=[
            jax.ShapeDtypeStruct((T, D), F32),
            jax.ShapeDtypeStruct((T, D // 2), jnp.uint32),
            jax.ShapeDtypeStruct((T, LANES), F32),
            jax.ShapeDtypeStruct((T, LANES), jnp.int32),
        ],
        compiler_params=_cparams(("parallel",)),
        name="out_proj_router",
    )(x2d, o_att, o_ssm, w_out_b, norm_w.reshape(1, D), w_router_b, b_router_p)


def _moe_kernel(be_ref, br_ref, nu_ref, x_ref, wg_ref, wu_ref, bg_ref, bu_ref, wd_ref, bd_ref, y_ref,
                *scratch, sub):
    i = pl.program_id(0)
    f = pl.program_id(1)
    rows = br_ref[i]
    bm, half = x_ref.shape

    def unpack(xp):
        lo = lax.bitcast_convert_type(lax.shift_left(xp, jnp.uint32(16)), F32)
        hi = lax.bitcast_convert_type(xp & jnp.uint32(0xFFFF0000), F32)
        return jnp.concatenate([lo.astype(BF16), hi.astype(BF16)], axis=1)

    if scratch:
        xb_ref, = scratch

        @pl.when((f == 0) & (rows > 0))
        def _():
            xb_ref[...] = unpack(x_ref[...])

    for r in range(bm // sub):
        rs = slice(r * sub, (r + 1) * sub)

        @pl.when(f == 0)
        def _():
            live = jnp.where(rows > r * sub, 1.0, 0.0)
            y_ref[rs, :] = jnp.broadcast_to(bd_ref[0] * live, (sub, y_ref.shape[1]))

        @pl.when(rows > r * sub)
        def _():
            x = xb_ref[rs, :] if scratch else unpack(x_ref[rs, :])
            gate = jnp.minimum(_dot(x, wg_ref[0].astype(BF16)) + bg_ref[0], SWIGLU_LIMIT)
            up = jnp.clip(_dot(x, wu_ref[0].astype(BF16)) + bu_ref[0], -SWIGLU_LIMIT, SWIGLU_LIMIT)
            act = (gate * jax.nn.sigmoid(SWIGLU_ALPHA * gate) * (up + 1.0)).astype(BF16)
            y_ref[rs, :] += _dot(act, wd_ref[0].astype(BF16))


def _moe_ffn_blocks(x_sorted, block_e, block_rows, n_used, w_gate_up_l, b_gate_up_l, w_down_l, b_down_l, bm,
                    tf_pref=512, n_sub=2, stage_x=False):
    R, half = x_sorted.shape
    D = 2 * half
    E, _, two_f = w_gate_up_l.shape
    d_ff = two_f // 2
    tf = _tile(d_ff, tf_pref)
    n_f = d_ff // tf
    n_blocks = R // bm
    sub = bm // n_sub
    bgu = b_gate_up_l.reshape(E, 1, two_f)
    bd = b_down_l.reshape(E, 1, D)

    def fe(i, f, br):
        return jnp.where(br[i] > 0, f, n_f - 1)

    def xi(i, br, nu):
        return jnp.where(br[i] > 0, i, jnp.maximum(nu[0] - 1, 0))

    grid_spec = pltpu.PrefetchScalarGridSpec(
        num_scalar_prefetch=3,
        grid=(n_blocks, n_f),
        in_specs=[
            pl.BlockSpec((bm, half), lambda i, f, be, br, nu: (xi(i, br, nu), 0)),
            pl.BlockSpec((1, D, tf), lambda i, f, be, br, nu: (be[i], 0, fe(i, f, br))),
            pl.BlockSpec((1, D, tf), lambda i, f, be, br, nu: (be[i], 0, n_f + fe(i, f, br))),
            pl.BlockSpec((1, 1, tf), lambda i, f, be, br, nu: (be[i], 0, fe(i, f, br))),
            pl.BlockSpec((1, 1, tf), lambda i, f, be, br, nu: (be[i], 0, n_f + fe(i, f, br))),
            pl.BlockSpec((1, tf, D), lambda i, f, be, br, nu: (be[i], fe(i, f, br), 0)),
            pl.BlockSpec((1, 1, D), lambda i, f, be, br, nu: (be[i], 0, 0)),
        ],
        out_specs=pl.BlockSpec((bm, D), lambda i, f, be, br, nu: (i, 0)),
        scratch_shapes=[pltpu.VMEM((bm, D), BF16)] if stage_x else [],
    )
    return pl.pallas_call(
        functools.partial(_moe_kernel, sub=sub),
        grid_spec=grid_spec,
        out_shape=jax.ShapeDtypeStruct((R, D), F32),
        compiler_params=_cparams(("parallel", "arbitrary")),
        name="moe_ffn",
    )(block_e, block_rows, n_used, x_sorted, w_gate_up_l, w_gate_up_l, bgu, bgu, w_down_l, bd)


def _moe_route(top_e, n_experts, bm):
    T, K = top_e.shape
    M = T * K
    n_blocks = (M + n_experts * (bm - 1)) // bm
    flat_e = top_e.reshape(-1)
    order = jnp.argsort(flat_e)
    rank = jnp.argsort(order)
    counts = jnp.bincount(flat_e, length=n_experts)
    padded = ((counts + bm - 1) // bm) * bm
    start = jnp.cumsum(counts) - counts
    pend = jnp.cumsum(padded)
    pstart = pend - padded
    pad_off = pstart - start
    slot_of = (rank + pad_off[flat_e]).astype(jnp.int32)
    slots = jnp.arange(n_blocks * bm)
    slot_e = jnp.minimum(jnp.sum(slots[:, None] >= pend[None, :], axis=1), n_experts - 1)
    real = (slots - pstart[slot_e]) < counts[slot_e]
    src = jnp.clip(slots - pad_off[slot_e], 0, M - 1)
    slot_tok = jnp.where(real, order[src] // K, slots % T).astype(jnp.int32)
    n_used = pend[-1] // bm
    blk = jnp.arange(n_blocks)
    block_e = slot_e[::bm]
    block_rows = jnp.clip(counts[block_e] - (blk * bm - pstart[block_e]), 0, bm)
    block_rows = jnp.where(blk < n_used, block_rows, 0).astype(jnp.int32)
    last_e = block_e[jnp.maximum(n_used - 1, 0)]
    block_e = jnp.where(blk < n_used, block_e, last_e).astype(jnp.int32)
    return slot_tok, block_e, block_rows, n_used.astype(jnp.int32).reshape(1), slot_of.reshape(T, K)


def _combine_kernel(x_ref, *refs, final):
    yg_refs = refs[:TOP_K]
    gate_ref, nw_ref, out_ref = refs[TOP_K:]
    gate = gate_ref[...]
    y = x_ref[...]
    for k in range(TOP_K):
        y = y + gate[:, k:k + 1] * yg_refs[k][0]
    if final:
        ms = jnp.mean(y * y, axis=-1, keepdims=True)
        y = (y * lax.rsqrt(ms + EPS)) * nw_ref[...]
    out_ref[...] = y


def _combine(x1, yg, row0, gates, norm_w, final):
    T, D = x1.shape
    bm = _tile(math.gcd(T, row0) if row0 else T, 256)
    off = row0 // bm

    def yg_spec(k):
        return pl.BlockSpec((1, bm, D), lambda i: (k, off + i, 0))

    return pl.pallas_call(
        functools.partial(_combine_kernel, final=final),
        grid=(T // bm,),
        in_specs=[pl.BlockSpec((bm, D), lambda i: (i, 0))] + [yg_spec(k) for k in range(TOP_K)] + [
            pl.BlockSpec((bm, LANES), lambda i: (i, 0)),
            pl.BlockSpec((1, D), lambda i: (0, 0)),
        ],
        out_specs=pl.BlockSpec((bm, D), lambda i: (i, 0)),
        out_shape=jax.ShapeDtypeStruct((T, D), F32),
        compiler_params=_cparams(("parallel",)),
        name="moe_combine_norm",
    )(x1, *([yg] * TOP_K), gates, norm_w.reshape(1, D))


def kernel(x_prompt, x_sample, cache_k, cache_v, page_table, state_ssm, state_conv, meta_tokens,
           norm_mix_w, w_in, lambda_q1, lambda_k1, lambda_q2, lambda_k2, subln_w, conv_w, conv_b,
           dt_bias, a_log, d_skip, ssm_norm_w, w_out, norm_ffn_w, w_router, b_router,
           w_gate_up, b_gate_up, w_down, b_down, norm_final_w):
    depth = w_in.shape[0]
    Bp, Lp, D = x_prompt.shape
    Bs, Ss, _ = x_sample.shape
    n_heads_att = cache_v.shape[3]
    att_w = n_heads_att * ATT_V_DIM
    qk_w = n_heads_att * 2 * ATT_HEAD_DIM
    n_heads_ssm = state_ssm.shape[2]
    ssm_w = n_heads_ssm * SSM_HEAD_DIM
    conv_dim = state_conv.shape[-1]
    n_main = 2 * qk_w + att_w + ssm_w + conv_dim
    n_experts = w_router.shape[-1]
    n_pool, page = cache_k.shape[1], cache_k.shape[2]
    Tp, Ts = Bp * Lp, Bs * Ss
    Tsm = Ts + N_META
    chunk = 128

    xp = x_prompt.reshape(Tp, D)
    xsm = jnp.concatenate([x_sample.reshape(Ts, D), meta_tokens.astype(F32)], axis=0)

    outs = {k: [] for k in ("kp", "vp", "ks", "vs", "ssmp", "convp", "ssms", "convs")}
    for layer in range(depth):
        lam_init = _lambda_init(layer)
        lam_vec = jnp.stack([lambda_q1[layer], lambda_k1[layer], lambda_q2[layer], lambda_k2[layer]])
        w_main = w_in[layer][:, :n_main].astype(BF16)
        w_dt = jnp.pad(w_in[layer][:, n_main:], ((0, 0), (0, LANES - n_heads_ssm))).astype(BF16)
        pad_h = (0, LANES - n_heads_ssm)
        ssd_p = {
            "conv_w": conv_w[layer], "conv_b": conv_b[layer].reshape(1, conv_dim),
            "dt_bias": jnp.pad(dt_bias[layer], pad_h).reshape(1, LANES),
            "dt_bias_t": dt_bias[layer].reshape(n_heads_ssm, 1),
            "a_log": jnp.pad(a_log[layer], pad_h).reshape(1, LANES),
            "a_log_t": a_log[layer].reshape(n_heads_ssm, 1),
            "d_skip_x": jnp.repeat(d_skip[layer], SSM_HEAD_DIM).reshape(1, ssm_w),
            "norm_w": ssm_norm_w[layer].reshape(1, ssm_w),
            "expand": (jnp.arange(LANES)[:, None] == (jnp.arange(ssm_w) // SSM_HEAD_DIM)[None, :]).astype(BF16),
        }

        proj_p, dt_p = _rms_proj(xp, norm_mix_w[layer], w_main, w_dt)
        proj_sm, dt_sm = _rms_proj(xsm, norm_mix_w[layer], w_main, w_dt)
        proj_p3 = proj_p.reshape(Bp, Lp, n_main)
        proj_s3 = proj_sm[:Ts].reshape(Bs, Ss, n_main)
        proj_m = proj_sm[Ts:]

        def pad_rows(a, rows):
            return jnp.pad(a, ((0, 0), (0, rows - a.shape[1]), (0, 0)))

        proj_m_pad = pad_rows(proj_m[None], chunk)
        oa_m = _flash_attn(proj_m_pad, None, 0, lam_vec, subln_w[layer], lam_init, n_heads_att)
        ck = jnp.transpose(cache_k[layer], (0, 2, 3, 4, 1)).reshape(n_pool, qk_w, page)
        cv = cache_v[layer].reshape(n_pool, page * n_heads_att, ATT_V_DIM)
        oa_p, oa_s = _attn_fused(proj_p3, proj_m_pad[0], N_META, proj_s3, ck, cv, page_table, lam_vec,
                                 subln_w[layer], lam_init, n_heads_att)

        zero_conv = jnp.zeros((1, CONV_PAD, conv_dim), F32)
        zero_state = jnp.zeros((1, SSM_GROUPS, SSM_STATE, ssm_w // SSM_GROUPS), F32)
        ys_m, conv_m, st_m = _ssd(proj_m_pad, pad_rows(dt_sm[Ts:][None], chunk),
                                  zero_conv, zero_state, True, N_META, ssd_p, n_heads_ssm)
        ys_p, conv_p, st_p = _ssd(proj_p3, dt_p.reshape(Bp, Lp, LANES), conv_m, st_m, True, chunk,
                                  ssd_p, n_heads_ssm)
        conv0_s = jnp.pad(state_conv[layer].astype(F32), ((0, 0), (CONV_PAD - (CONV_WIDTH - 1), 0), (0, 0)))
        ys_s, conv_s, st_s = _ssd(pad_rows(proj_s3, chunk), pad_rows(dt_sm[:Ts].reshape(Bs, Ss, LANES), chunk),
                                  conv0_s, _state_to_t(state_ssm[layer].astype(F32), n_heads_ssm),
                                  False, Ss, ssd_p, n_heads_ssm)

        w_out_b = w_out[layer].astype(BF16)
        w_router_b = jnp.pad(w_router[layer], ((0, 0), (0, LANES - n_experts))).astype(BF16)
        b_router_p = jnp.pad(b_router[layer].astype(F32), (0, LANES - n_experts),
                             constant_values=-1e30).reshape(1, LANES)
        oa_sm = jnp.concatenate([oa_s.reshape(Ts, att_w).astype(BF16), oa_m[0, :N_META]], axis=0)
        os_sm = jnp.concatenate([ys_s[:, :Ss].reshape(Ts, ssm_w), ys_m[0, :N_META]], axis=0)
        x1_p, h2_p, gate_p, idx_p = _out_router(xp, oa_p.reshape(Tp, att_w), ys_p.reshape(Tp, ssm_w),
                                                w_out_b, norm_ffn_w[layer], w_router_b, b_router_p)
        x1_sm, h2_sm, gate_sm, idx_sm = _out_router(xsm, oa_sm, os_sm, w_out_b, norm_ffn_w[layer],
                                                    w_router_b, b_router_p)

        h2 = jnp.concatenate([h2_p, h2_sm], axis=0)
        top_e = jnp.concatenate([idx_p[:, :TOP_K], idx_sm[:, :TOP_K]], axis=0)
        T_all = Tp + Tsm
        bm_moe = 1024 if T_all * TOP_K >= 1024 * n_experts else 256
        slot_tok, block_e, block_rows, n_used, slot_of = _moe_route(top_e, n_experts, bm_moe)
        x_sorted = h2[slot_tok]
        yb = _moe_ffn_blocks(x_sorted, block_e, block_rows, n_used, w_gate_up[layer], b_gate_up[layer],
                             w_down[layer], b_down[layer], bm_moe)
        yg = yb[slot_of.T.reshape(-1)].reshape(TOP_K, T_all, D)
        final = layer == depth - 1
        xp = _combine(x1_p, yg, 0, gate_p, norm_final_w, final)
        xsm = _combine(x1_sm, yg, Tp, gate_sm, norm_final_w, final)

        k_m = jnp.broadcast_to(proj_m[None, :, qk_w:2 * qk_w], (Bp, N_META, qk_w))
        v_m = jnp.broadcast_to(proj_m[None, :, 2 * qk_w:2 * qk_w + att_w], (Bp, N_META, att_w))
        k_p = jnp.concatenate([k_m, proj_p3[:, :, qk_w:2 * qk_w]], axis=1)
        v_p = jnp.concatenate([v_m, proj_p3[:, :, 2 * qk_w:2 * qk_w + att_w]], axis=1)
        outs["kp"].append(k_p.reshape(Bp, N_META + Lp, n_heads_att, 2, ATT_HEAD_DIM))
        outs["vp"].append(v_p.reshape(Bp, N_META + Lp, n_heads_att, ATT_V_DIM))
        outs["ks"].append(proj_s3[:, :, qk_w:2 * qk_w].reshape(Bs, Ss, n_heads_att, 2, ATT_HEAD_DIM))
        outs["vs"].append(proj_s3[:, :, 2 * qk_w:2 * qk_w + att_w].reshape(Bs, Ss, n_heads_att, ATT_V_DIM))
        outs["ssmp"].append(_state_from_t(st_p, n_heads_ssm))
        outs["convp"].append(conv_p[:, CONV_PAD - (CONV_WIDTH - 1):])
        outs["ssms"].append(_state_from_t(st_s, n_heads_ssm))
        outs["convs"].append(conv_s[:, CONV_PAD - (CONV_WIDTH - 1):])

    y_prompt = xp.reshape(Bp, Lp, D)
    y_sample = xsm[:Ts].reshape(Bs, Ss, D)
    return (y_prompt, y_sample, jnp.stack(outs["kp"]), jnp.stack(outs["vp"]), jnp.stack(outs["ks"]),
            jnp.stack(outs["vs"]), jnp.stack(outs["ssmp"]), jnp.stack(outs["convp"]),
            jnp.stack(outs["ssms"]), jnp.stack(outs["convs"]))
```

```python
import functools
import math

import jax
import jax.numpy as jnp
from jax import lax
from jax.experimental import pallas as pl
from jax.experimental.pallas import tpu as pltpu

F32 = jnp.float32
BF16 = jnp.bfloat16

N_META = 16
ATT_HEAD_DIM = 64
ATT_V_DIM = 128
SSM_HEAD_DIM = 64
SSM_GROUPS = 2
SSM_STATE = 128
CONV_WIDTH = 4
TOP_K = 4
SWIGLU_ALPHA = 1.702
SWIGLU_LIMIT = 7.0
EPS = 1e-5
ATT_SCALE = ATT_HEAD_DIM ** -0.5

LANES = 128
SUBLANES = 8
VMEM_LIMIT_BYTES = 56 * 1024 * 1024
NEG_INF = float("-inf")
CONV_PAD = SUBLANES


def _tile(n, pref):
    if n <= pref:
        return n
    for t in range(pref, 0, -1):
        if n % t == 0 and t % SUBLANES == 0:
            return t
    return n


def _cparams(sem):
    return pltpu.CompilerParams(dimension_semantics=sem, vmem_limit_bytes=VMEM_LIMIT_BYTES)


def _lambda_init(layer):
    return 0.8 - 0.6 * math.exp(-0.3 * layer)


def _lam_from_ref(lam_ref, lam_init):
    lv = lam_ref[...]
    s1 = jnp.sum(lv[0:1] * lv[1:2], axis=-1, keepdims=True)
    s2 = jnp.sum(lv[2:3] * lv[3:4], axis=-1, keepdims=True)
    return jnp.exp(s1) - jnp.exp(s2) + lam_init


def _split3(x):
    hi = x.astype(BF16)
    r1 = x - hi.astype(F32)
    mid = r1.astype(BF16)
    lo = (r1 - mid.astype(F32)).astype(BF16)
    return hi, mid, lo


def _dot(a, b):
    return jnp.dot(a, b, preferred_element_type=F32)


def _dot_nt(a, b):
    return lax.dot_general(a, b, (((1,), (1,)), ((), ())), preferred_element_type=F32)


def _rms_proj_kernel(x_ref, nw_ref, w_ref, wdt_ref, proj_ref, dt_ref, h_ref):
    @pl.when(pl.program_id(1) == 0)
    def _():
        x = x_ref[...]
        ms = jnp.mean(x * x, axis=-1, keepdims=True)
        h = ((x * lax.rsqrt(ms + EPS)) * nw_ref[...]).astype(BF16)
        h_ref[...] = h
        dt_ref[...] = _dot(h, wdt_ref[...])

    proj_ref[...] = _dot(h_ref[...], w_ref[...])


def _rms_proj(x2d, norm_w, w_main, w_dt):
    T, D = x2d.shape
    N = w_main.shape[1]
    bm = _tile(T, 1024)
    bn = _tile(N, 512)
    return pl.pallas_call(
        _rms_proj_kernel,
        grid=(T // bm, N // bn),
        in_specs=[
            pl.BlockSpec((bm, D), lambda i, j: (i, 0)),
            pl.BlockSpec((1, D), lambda i, j: (0, 0)),
            pl.BlockSpec((D, bn), lambda i, j: (0, j)),
            pl.BlockSpec((D, LANES), lambda i, j: (0, 0)),
        ],
        out_specs=[
            pl.BlockSpec((bm, bn), lambda i, j: (i, j)),
            pl.BlockSpec((bm, LANES), lambda i, j: (i, 0)),
        ],
        out_shape=[jax.ShapeDtypeStruct((T, N), F32), jax.ShapeDtypeStruct((T, LANES), F32)],
        scratch_shapes=[pltpu.VMEM((bm, D), BF16)],
        compiler_params=_cparams(("parallel", "arbitrary")),
        name="rms_proj",
    )(x2d, norm_w.reshape(1, D), w_main, w_dt)


def _flash_kernel(*refs, tq, n_prefix, lam_init):
    if n_prefix:
        (qi_tab, ki_tab, lam_ref, sub_ref, q_ref, k_ref, v_ref, pk_ref, pv_ref,
         o_ref, m_ref, l_ref, acc_ref) = refs
    else:
        (qi_tab, ki_tab, lam_ref, sub_ref, q_ref, k_ref, v_ref,
         o_ref, m_ref, l_ref, acc_ref) = refs
        pk_ref = pv_ref = None
    t = pl.program_id(2)
    _flash_body(lam_ref, sub_ref, q_ref, k_ref, v_ref, pk_ref, pv_ref, o_ref, m_ref, l_ref, acc_ref,
                qi_tab[t], ki_tab[t], tq=tq, n_prefix=n_prefix, lam_init=lam_init)


def _flash_body(lam_ref, sub_ref, q_ref, k_ref, v_ref, pk_ref, pv_ref, o_ref, m_ref, l_ref, acc_ref,
                qi, ki, *, tq, n_prefix, lam_init):
    hps = q_ref.shape[-1] // LANES
    lane = lax.broadcasted_iota(jnp.int32, (tq, LANES), 1)
    q_maps = []
    for u in range(hps):
        q = q_ref[0, :, u * LANES:(u + 1) * LANES] * ATT_SCALE
        q_maps.append((jnp.where(lane < ATT_HEAD_DIM, q, 0.0).astype(BF16),
                       jnp.where(lane >= ATT_HEAD_DIM, q, 0.0).astype(BF16)))

    def update(k, v, mask):
        reps = k.shape[0] // LANES
        for u in range(hps):
            kb = k[:, u * LANES:(u + 1) * LANES].astype(BF16)
            vb = v[:, u * LANES:(u + 1) * LANES].astype(BF16)
            for c in range(2):
                i = 2 * u + c
                s = _dot_nt(q_maps[u][c], kb)
                if mask is not None:
                    s = jnp.where(mask, s, NEG_INF)
                m_prev = m_ref[i]
                m_new = jnp.maximum(m_prev, jnp.max(s, axis=-1, keepdims=True))
                corr = jnp.exp(m_prev - m_new)
                p = jnp.exp(s - jnp.concatenate([m_new] * reps, axis=1))
                part = p[:, 0:LANES]
                for j in range(1, reps):
                    part = part + p[:, j * LANES:(j + 1) * LANES]
                l_ref[i] = l_ref[i] * corr + part
                acc_ref[i] = acc_ref[i] * corr + _dot(p.astype(BF16), vb)
                m_ref[i] = m_new

    @pl.when(ki == 0)
    def _():
        m_ref[...] = jnp.full(m_ref.shape, NEG_INF, F32)
        l_ref[...] = jnp.zeros(l_ref.shape, F32)
        acc_ref[...] = jnp.zeros(acc_ref.shape, F32)
        if n_prefix:
            col = lax.broadcasted_iota(jnp.int32, (tq, pk_ref.shape[0]), 1)
            update(pk_ref[...], pv_ref[...], col < n_prefix)

    @pl.when(ki < qi)
    def _():
        update(k_ref[0], v_ref[0], None)

    @pl.when(ki == qi)
    def _():
        row = lax.broadcasted_iota(jnp.int32, (tq, tq), 0)
        col = lax.broadcasted_iota(jnp.int32, (tq, tq), 1)
        update(k_ref[0], v_ref[0], row >= col)
        lam = _lam_from_ref(lam_ref, lam_init)
        for u in range(hps):
            l0 = jnp.sum(l_ref[2 * u], axis=-1, keepdims=True)
            l1 = jnp.sum(l_ref[2 * u + 1], axis=-1, keepdims=True)
            o = acc_ref[2 * u] / l0 - lam * (acc_ref[2 * u + 1] / l1)
            ms = jnp.mean(o * o, axis=-1, keepdims=True)
            y = (o * lax.rsqrt(ms + EPS)) * sub_ref[...]
            o_ref[0, :, u * LANES:(u + 1) * LANES] = (y * (1.0 - lam_init)).astype(o_ref.dtype)


def _flash_attn(proj3, prefix, n_prefix, lam_vec, subln_w, lam_init, n_heads, out_dtype=BF16):
    B, L, _ = proj3.shape
    assert L % LANES == 0
    tq = _tile(L, 512)
    nq = L // tq
    pairs = [(a, b) for a in range(nq) for b in range(a + 1)]
    qi_tab = jnp.asarray([p[0] for p in pairs], jnp.int32)
    ki_tab = jnp.asarray([p[1] for p in pairs], jnp.int32)
    H = n_heads
    has_prefix = prefix is not None
    if not has_prefix:
        n_prefix = 0
    in_specs = [
        pl.BlockSpec((4, ATT_HEAD_DIM), lambda b, h, t, qt, kt: (0, 0)),
        pl.BlockSpec((1, ATT_V_DIM), lambda b, h, t, qt, kt: (0, 0)),
        pl.BlockSpec((1, tq, LANES), lambda b, h, t, qt, kt: (b, qt[t], h)),
        pl.BlockSpec((1, tq, LANES), lambda b, h, t, qt, kt: (b, kt[t], H + h)),
        pl.BlockSpec((1, tq, LANES), lambda b, h, t, qt, kt: (b, kt[t], 2 * H + h)),
    ]
    args = [lam_vec, subln_w.reshape(1, ATT_V_DIM), proj3, proj3, proj3]
    if has_prefix:
        P = prefix.shape[0]
        in_specs += [
            pl.BlockSpec((P, LANES), lambda b, h, t, qt, kt: (0, H + h)),
            pl.BlockSpec((P, LANES), lambda b, h, t, qt, kt: (0, 2 * H + h)),
        ]
        args += [prefix, prefix]
    grid_spec = pltpu.PrefetchScalarGridSpec(
        num_scalar_prefetch=2,
        grid=(B, H, len(pairs)),
        in_specs=in_specs,
        out_specs=pl.BlockSpec((1, tq, LANES), lambda b, h, t, qt, kt: (b, qt[t], h)),
        scratch_shapes=[
            pltpu.VMEM((2, tq, LANES), F32),
            pltpu.VMEM((2, tq, LANES), F32),
            pltpu.VMEM((2, tq, ATT_V_DIM), F32),
        ],
    )
    return pl.pallas_call(
        functools.partial(_flash_kernel, tq=tq, n_prefix=n_prefix, lam_init=lam_init),
        grid_spec=grid_spec,
        out_shape=jax.ShapeDtypeStruct((B, L, H * ATT_V_DIM), out_dtype),
        compiler_params=_cparams(("parallel", "parallel", "arbitrary")),
        name="flash_diff_attn",
    )(qi_tab, ki_tab, *args)


def _sample_body(lam_ref, sub_ref, q_ref, kn_ref, vn_ref, k_refs, v_refs, o_ref, qbd_ref, m_ref, l_ref, acc_ref,
                 step, n_steps, *, n_heads, s_new, page, lam_init):
    n_rows = n_heads * 2 * s_new
    width = n_heads * 2 * ATT_HEAD_DIM

    def update(ss, v_head):
        m_prev = m_ref[...]
        m_loc = [jnp.max(s, axis=-1, keepdims=True) for s in ss]
        m_new = m_prev
        for m in m_loc:
            m_new = jnp.maximum(m_new, m)
        corr = jnp.exp(m_prev - m_new)
        alphas = [jnp.exp(m - m_new) for m in m_loc]
        ps = [jnp.exp(s - m) for s, m in zip(ss, m_loc)]
        l_new = l_ref[...] * corr
        for p, a in zip(ps, alphas):
            l_new = l_new + a * jnp.sum(p, axis=-1, keepdims=True)
        l_ref[...] = l_new
        m_ref[...] = m_new
        pbs = [p.astype(BF16) for p in ps]
        for h in range(n_heads):
            r0, r1 = h * 2 * s_new, (h + 1) * 2 * s_new
            pv = acc_ref[r0:r1, :] * corr[r0:r1]
            for j, pb in enumerate(pbs):
                pv = pv + alphas[j][r0:r1] * _dot(pb[r0:r1, :], v_head(j, h))
            acc_ref[r0:r1, :] = pv

    @pl.when(step == 0)
    def _():
        q = q_ref[0] * ATT_SCALE
        qt = jnp.concatenate([q] * (2 * n_heads), axis=0)
        row_hc = lax.broadcasted_iota(jnp.int32, (n_rows, width), 0) // s_new
        col_hc = lax.broadcasted_iota(jnp.int32, (n_rows, width), 1) // ATT_HEAD_DIM
        qbd = jnp.where(row_hc == col_hc, qt, 0.0).astype(BF16)
        qbd_ref[...] = qbd
        m_ref[...] = jnp.full(m_ref.shape, NEG_INF, F32)
        l_ref[...] = jnp.zeros(l_ref.shape, F32)
        acc_ref[...] = jnp.zeros(acc_ref.shape, F32)
        pad = jnp.zeros((page - s_new, width), F32)
        k_self = jnp.concatenate([kn_ref[0], pad], axis=0).astype(BF16)
        v_self = jnp.concatenate([vn_ref[0], pad], axis=0).astype(BF16)
        r = lax.broadcasted_iota(jnp.int32, (n_rows, page), 0)
        c = lax.broadcasted_iota(jnp.int32, (n_rows, page), 1)
        s_self = jnp.where(c <= (r % s_new), _dot_nt(qbd, k_self), NEG_INF)
        update([s_self], lambda j, h: v_self[:, h * ATT_V_DIM:(h + 1) * ATT_V_DIM])

    qbd = qbd_ref[...]
    update([_dot(qbd, k[0].astype(BF16)) for k in k_refs],
           lambda j, h: v_refs[j][0, pl.ds(h, page, stride=n_heads), :].astype(BF16))

    @pl.when(step == n_steps - 1)
    def _():
        lam = _lam_from_ref(lam_ref, lam_init)
        o_all = acc_ref[...] / l_ref[...]
        for h in range(n_heads):
            r0 = h * 2 * s_new
            o = o_all[r0:r0 + s_new] - lam * o_all[r0 + s_new:r0 + 2 * s_new]
            ms = jnp.mean(o * o, axis=-1, keepdims=True)
            y = (o * lax.rsqrt(ms + EPS)) * sub_ref[...]
            o_ref[0, :, h * ATT_V_DIM:(h + 1) * ATT_V_DIM] = y * (1.0 - lam_init)


def _attn_fused_kernel(*refs, pps, spb, n_s, tq, n_prefix, n_heads, s_new, page, lam_init):
    b_tab, h_tab, qi_tab, ki_tab, pt_ref = refs[:5]
    lam_ref, sub_ref, q_ref, k_ref, v_ref, pk_ref, pv_ref, qn_ref, kn_ref, vn_ref = refs[5:15]
    k_refs = refs[15:15 + pps]
    v_refs = refs[15 + pps:15 + 2 * pps]
    o_ref, os_ref, m_ref, l_ref, acc_ref, qbd_ref, ms_ref, ls_ref, accs_ref = refs[15 + 2 * pps:]
    g = pl.program_id(0)
    _flash_body(lam_ref, sub_ref, q_ref, k_ref, v_ref, pk_ref, pv_ref, o_ref, m_ref, l_ref, acc_ref,
                qi_tab[g], ki_tab[g], tq=tq, n_prefix=n_prefix, lam_init=lam_init)

    @pl.when(g < n_s)
    def _():
        _sample_body(lam_ref, sub_ref, qn_ref, kn_ref, vn_ref, k_refs, v_refs, os_ref, qbd_ref, ms_ref,
                     ls_ref, accs_ref, g % spb, spb, n_heads=n_heads, s_new=s_new, page=page,
                     lam_init=lam_init)


def _attn_fused(proj3, prefix, n_prefix, proj_s3, cache_k_l, cache_v_l, page_table, lam_vec, subln_w,
                lam_init, n_heads):
    B, L, _ = proj3.shape
    assert L % LANES == 0
    tq = _tile(L, 512)
    nq = L // tq
    hps = 2 if n_heads % 2 == 0 else 1
    H = n_heads // hps
    W = hps * LANES
    steps = [(b, h, a, c) for b in range(B) for h in range(H) for a in range(nq) for c in range(a + 1)]
    G = len(steps)
    tabs = [jnp.asarray([s[i] for s in steps], jnp.int32) for i in range(4)]
    bs, S, _ = proj_s3.shape
    n_pool, width, page = cache_k_l.shape
    n_pages = page_table.shape[1]
    assert bs <= G
    pps = next(p for p in range(1, n_pages + 1) if n_pages % p == 0 and bs * (n_pages // p) <= G)
    spb = n_pages // pps
    n_s = bs * spb
    n_rows = n_heads * 2 * S
    P = prefix.shape[0]

    def sb(g):
        return jnp.minimum(g, n_s - 1) // spb

    def pg(g):
        return jnp.minimum(g, n_s - 1) % spb

    def k_spec(j):
        return pl.BlockSpec((1, width, page), lambda g, bt, ht, qt, kt, pt: (pt[sb(g), pg(g) * pps + j], 0, 0))

    def v_spec(j):
        return pl.BlockSpec((1, page * n_heads, ATT_V_DIM),
                            lambda g, bt, ht, qt, kt, pt: (pt[sb(g), pg(g) * pps + j], 0, 0))

    in_specs = [
        pl.BlockSpec((4, ATT_HEAD_DIM), lambda g, bt, ht, qt, kt, pt: (0, 0)),
        pl.BlockSpec((1, ATT_V_DIM), lambda g, bt, ht, qt, kt, pt: (0, 0)),
        pl.BlockSpec((1, tq, W), lambda g, bt, ht, qt, kt, pt: (bt[g], qt[g], ht[g])),
        pl.BlockSpec((1, tq, W), lambda g, bt, ht, qt, kt, pt: (bt[g], kt[g], H + ht[g])),
        pl.BlockSpec((1, tq, W), lambda g, bt, ht, qt, kt, pt: (bt[g], kt[g], 2 * H + ht[g])),
        pl.BlockSpec((P, W), lambda g, bt, ht, qt, kt, pt: (0, H + ht[g])),
        pl.BlockSpec((P, W), lambda g, bt, ht, qt, kt, pt: (0, 2 * H + ht[g])),
        pl.BlockSpec((1, S, width), lambda g, bt, ht, qt, kt, pt: (sb(g), 0, 0)),
        pl.BlockSpec((1, S, width), lambda g, bt, ht, qt, kt, pt: (sb(g), 0, 1)),
        pl.BlockSpec((1, S, width), lambda g, bt, ht, qt, kt, pt: (sb(g), 0, 2)),
    ] + [k_spec(j) for j in range(pps)] + [v_spec(j) for j in range(pps)]
    grid_spec = pltpu.PrefetchScalarGridSpec(
        num_scalar_prefetch=5,
        grid=(G,),
        in_specs=in_specs,
        out_specs=[
            pl.BlockSpec((1, tq, W), lambda g, bt, ht, qt, kt, pt: (bt[g], qt[g], ht[g])),
            pl.BlockSpec((1, S, width), lambda g, bt, ht, qt, kt, pt: (sb(g), 0, 0)),
        ],
        scratch_shapes=[
            pltpu.VMEM((2 * hps, tq, LANES), F32),
            pltpu.VMEM((2 * hps, tq, LANES), F32),
            pltpu.VMEM((2 * hps, tq, ATT_V_DIM), F32),
            pltpu.VMEM((n_rows, width), BF16),
            pltpu.VMEM((n_rows, 1), F32),
            pltpu.VMEM((n_rows, 1), F32),
            pltpu.VMEM((n_rows, ATT_V_DIM), F32),
        ],
    )
    return pl.pallas_call(
        functools.partial(_attn_fused_kernel, pps=pps, spb=spb, n_s=n_s, tq=tq, n_prefix=n_prefix,
                          n_heads=n_heads, s_new=S, page=page, lam_init=lam_init),
        grid_spec=grid_spec,
        out_shape=[jax.ShapeDtypeStruct((B, L, n_heads * ATT_V_DIM), BF16),
                   jax.ShapeDtypeStruct((bs, S, width), F32)],
        compiler_params=_cparams(("arbitrary",)),
        name="attn_fused",
    )(*tabs, page_table, lam_vec, subln_w.reshape(1, ATT_V_DIM), proj3, proj3, proj3, prefix, prefix,
      proj_s3, proj_s3, proj_s3, *([cache_k_l] * pps), *([cache_v_l] * pps))


def _ssd_kernel(z_ref, xa_ref, xb_ref, bc_ref, dt_ref, dtt_ref, conv0_ref, s0_ref,
                cw_ref, cb_ref, dtb_ref, dtbt_ref, alog_ref, alogt_ref, dsk_ref, nw_ref, ex_ref,
                y_ref, convo_ref, so_ref, buf_ref, st_ref, *, Q, valid, n_heads):
    c = pl.program_id(1)
    n_c = pl.num_programs(1)
    ssm_w = n_heads * SSM_HEAD_DIM
    gw = ssm_w // SSM_GROUPS
    hg = n_heads // SSM_GROUPS

    @pl.when(c == 0)
    def _():
        buf_ref[0:CONV_PAD, :] = conv0_ref[0]
        st_ref[...] = s0_ref[0]

    buf_ref[CONV_PAD:CONV_PAD + Q, 0:gw] = xa_ref[0]
    buf_ref[CONV_PAD:CONV_PAD + Q, gw:2 * gw] = xb_ref[0]
    buf_ref[CONV_PAD:CONV_PAD + Q, 2 * gw:3 * gw] = bc_ref[0]
    base = CONV_PAD - (CONV_WIDTH - 1)
    conv = cb_ref[...]
    for w in range(CONV_WIDTH):
        conv = conv + buf_ref[base + w:base + w + Q, :] * cw_ref[w:w + 1, :]
    xbc = conv * jax.nn.sigmoid(conv)
    xs = xbc[:, :ssm_w]
    bmat = xbc[:, ssm_w:ssm_w + SSM_GROUPS * SSM_STATE]
    cmat = xbc[:, ssm_w + SSM_GROUPS * SSM_STATE:]

    lane = lax.broadcasted_iota(jnp.int32, (1, LANES), 1)
    a_row = jnp.where(lane < n_heads, -jnp.exp(alog_ref[...]), 0.0)
    dt = jax.nn.softplus(dt_ref[0] + dtb_ref[...])
    a_col = -jnp.exp(alogt_ref[...])
    dtt = jax.nn.softplus(dtt_ref[0] + dtbt_ref[...])
    if valid < Q:
        dt = jnp.where(lax.broadcasted_iota(jnp.int32, dt.shape, 0) < valid, dt, 0.0)
        dtt = jnp.where(lax.broadcasted_iota(jnp.int32, dtt.shape, 1) < valid, dtt, 0.0)
    da = dt * a_row
    dat = dtt * a_col

    r_i = lax.broadcasted_iota(jnp.int32, (Q, Q), 0)
    c_i = lax.broadcasted_iota(jnp.int32, (Q, Q), 1)
    causal = r_i >= c_i
    tri_l = jnp.where(causal, 1.0, 0.0).astype(BF16)
    tri_u = jnp.where(r_i <= c_i, 1.0, 0.0).astype(BF16)
    da3 = _split3(da)
    a_cum = _dot(tri_l, da3[0]) + _dot(tri_l, da3[1]) + _dot(tri_l, da3[2])
    dat3 = _split3(dat)
    a_cum_t = _dot(dat3[0], tri_u) + _dot(dat3[1], tri_u) + _dot(dat3[2], tri_u)

    ex = ex_ref[...]
    dt3 = _split3(dt)
    dt_x = _dot(dt3[0], ex) + _dot(dt3[1], ex) + _dot(dt3[2], ex)
    ac3 = _split3(a_cum)
    acum_x = _dot(ac3[0], ex) + _dot(ac3[1], ex) + _dot(ac3[2], ex)
    a_last_x = acum_x[Q - 1:Q, :]
    e_in = jnp.exp(acum_x)
    dec_end = jnp.exp(a_last_x - acum_x)
    chunk_decay = jnp.exp(a_last_x)

    xdt = xs * dt_x
    xdt_b = xdt.astype(BF16)
    xdec_b = (xdt * dec_end).astype(BF16)
    lane_q = lax.broadcasted_iota(jnp.int32, (Q, LANES), 1)

    y_groups = []
    for g in range(SSM_GROUPS):
        cg = cmat[:, g * SSM_STATE:(g + 1) * SSM_STATE].astype(BF16)
        bg = bmat[:, g * SSM_STATE:(g + 1) * SSM_STATE]
        gmat = _dot_nt(cg, bg.astype(BF16))
        st_old = st_ref[g]
        y_off = _dot(cg, st_old.astype(BF16)) * e_in[:, g * gw:(g + 1) * gw]
        pieces = []
        for j in range(hg // 2):
            halves = []
            for u in range(2):
                hh = g * hg + 2 * j + u
                seg = jnp.exp(jnp.where(causal, a_cum[:, hh:hh + 1] - a_cum_t[hh:hh + 1, :], NEG_INF))
                mh = (gmat * seg).astype(BF16)
                blk = (g * hg + 2 * j) * SSM_HEAD_DIM
                halves.append(_dot(mh, xdt_b[:, blk:blk + LANES]))
            pieces.append(jnp.where(lane_q < SSM_HEAD_DIM, halves[0], halves[1]))
        y_diag = jnp.concatenate(pieces, axis=1)
        y_groups.append(y_diag + y_off)
        st_ref[g] = (st_old * chunk_decay[:, g * gw:(g + 1) * gw]
                     + _dot(bg.T.astype(BF16), xdec_b[:, g * gw:(g + 1) * gw]))
    y = jnp.concatenate(y_groups, axis=1) + dsk_ref[...] * xs

    zv = z_ref[0]
    y = y * (zv * jax.nn.sigmoid(zv))
    outs = []
    for g in range(SSM_GROUPS):
        yg = y[:, g * gw:(g + 1) * gw]
        ms = jnp.mean(yg * yg, axis=-1, keepdims=True)
        outs.append((yg * lax.rsqrt(ms + EPS)) * nw_ref[:, g * gw:(g + 1) * gw])
    y_ref[0] = jnp.concatenate(outs, axis=1).astype(y_ref.dtype)

    tail = buf_ref[valid:valid + CONV_PAD, :]
    buf_ref[0:CONV_PAD, :] = tail

    @pl.when(c == n_c - 1)
    def _():
        convo_ref[0] = tail
        so_ref[0] = st_ref[...]


def _ssd(proj3, dt3, conv0, s0t, shared_init, valid, p, n_heads):
    B, L, _ = proj3.shape
    ssm_w = n_heads * SSM_HEAD_DIM
    gw = ssm_w // SSM_GROUPS
    conv_dim = ssm_w + 2 * SSM_GROUPS * SSM_STATE
    Q = min(L, 128)
    n_c = L // Q
    if valid < Q:
        assert n_c == 1
    else:
        valid = Q
    dtt3 = jnp.swapaxes(dt3[:, :, :n_heads], 1, 2)
    z_blk = 3 * ssm_w // ssm_w
    x_blk = 4 * ssm_w // gw
    init_idx = (lambda b, c: (0, 0, 0)) if shared_init else (lambda b, c: (b, 0, 0))
    init_idx4 = (lambda b, c: (0, 0, 0, 0)) if shared_init else (lambda b, c: (b, 0, 0, 0))
    const2 = lambda b, c: (0, 0)
    in_specs = [
        pl.BlockSpec((1, Q, ssm_w), lambda b, c: (b, c, z_blk)),
        pl.BlockSpec((1, Q, gw), lambda b, c: (b, c, x_blk)),
        pl.BlockSpec((1, Q, gw), lambda b, c: (b, c, x_blk + 1)),
        pl.BlockSpec((1, Q, gw), lambda b, c: (b, c, x_blk + 2)),
        pl.BlockSpec((1, Q, LANES), lambda b, c: (b, c, 0)),
        pl.BlockSpec((1, n_heads, Q), lambda b, c: (b, 0, c)),
        pl.BlockSpec((1, CONV_PAD, conv_dim), init_idx),
        pl.BlockSpec((1, SSM_GROUPS, SSM_STATE, gw), init_idx4),
        pl.BlockSpec((CONV_WIDTH, conv_dim), const2),
        pl.BlockSpec((1, conv_dim), const2),
        pl.BlockSpec((1, LANES), const2),
        pl.BlockSpec((n_heads, 1), const2),
        pl.BlockSpec((1, LANES), const2),
        pl.BlockSpec((n_heads, 1), const2),
        pl.BlockSpec((1, ssm_w), const2),
        pl.BlockSpec((1, ssm_w), const2),
        pl.BlockSpec((LANES, ssm_w), const2),
    ]
    out_specs = [
        pl.BlockSpec((1, Q, ssm_w), lambda b, c: (b, c, 0)),
        pl.BlockSpec((1, CONV_PAD, conv_dim), lambda b, c: (b, 0, 0)),
        pl.BlockSpec((1, SSM_GROUPS, SSM_STATE, gw), lambda b, c: (b, 0, 0, 0)),
    ]
    out_shape = [
        jax.ShapeDtypeStruct((B, L, ssm_w), BF16),
        jax.ShapeDtypeStruct((B, CONV_PAD, conv_dim), F32),
        jax.ShapeDtypeStruct((B, SSM_GROUPS, SSM_STATE, gw), F32),
    ]
    return pl.pallas_call(
        functools.partial(_ssd_kernel, Q=Q, valid=valid, n_heads=n_heads),
        grid=(B, n_c),
        in_specs=in_specs,
        out_specs=out_specs,
        out_shape=out_shape,
        scratch_shapes=[
            pltpu.VMEM((Q + CONV_PAD, conv_dim), F32),
            pltpu.VMEM((SSM_GROUPS, SSM_STATE, gw), F32),
        ],
        compiler_params=_cparams(("parallel", "arbitrary")),
        name="conv_ssd",
    )(proj3, proj3, proj3, proj3, dt3, dtt3, conv0, s0t,
      p["conv_w"], p["conv_b"], p["dt_bias"], p["dt_bias_t"], p["a_log"], p["a_log_t"],
      p["d_skip_x"], p["norm_w"], p["expand"])


def _state_to_t(s, n_heads):
    B = s.shape[0]
    hg = n_heads // SSM_GROUPS
    s = s.reshape(B, SSM_GROUPS, hg * SSM_HEAD_DIM, SSM_STATE)
    return jnp.swapaxes(s, 2, 3)


def _state_from_t(st, n_heads):
    B = st.shape[0]
    return jnp.swapaxes(st, 2, 3).reshape(B, n_heads, SSM_HEAD_DIM, SSM_STATE)


def _out_router_kernel(x_ref, oa_ref, os_ref, wo_ref, nw_ref, wr_ref, br_ref,
                       x1_ref, h2_ref, gate_ref, idx_ref, *, att_w):
    x1 = x_ref[...] + _dot(oa_ref[...], wo_ref[0:att_w, :]) + _dot(os_ref[...], wo_ref[att_w:, :])
    x1_ref[...] = x1
    ms = jnp.mean(x1 * x1, axis=-1, keepdims=True)
    h2 = ((x1 * lax.rsqrt(ms + EPS)) * nw_ref[...]).astype(BF16)
    half = h2.shape[1] // 2
    h2f = h2.astype(F32)
    lo_bits = lax.bitcast_convert_type(h2f[:, :half], jnp.uint32)
    hi_bits = lax.bitcast_convert_type(h2f[:, half:], jnp.uint32)
    h2_ref[...] = hi_bits | lax.shift_right_logical(lo_bits, jnp.uint32(16))
    logits = _dot(h2, wr_ref[...]) + br_ref[...]
    lane = lax.broadcasted_iota(jnp.int32, logits.shape, 1)
    vals, idxs = [], []
    work = logits
    for _ in range(TOP_K):
        m = jnp.max(work, axis=-1, keepdims=True)
        i = jnp.min(jnp.where(work == m, lane, LANES), axis=-1, keepdims=True)
        vals.append(m)
        idxs.append(i)
        work = jnp.where(lane == i, NEG_INF, work)
    exps = [jnp.exp(v - vals[0]) for v in vals]
    denom = exps[0] + exps[1] + exps[2] + exps[3]
    gate = jnp.zeros(logits.shape, F32)
    idx = jnp.zeros(logits.shape, jnp.int32)
    for k in range(TOP_K):
        gate = jnp.where(lane == k, exps[k] / denom, gate)
        idx = jnp.where(lane == k, idxs[k], idx)
    gate_ref[...] = gate
    idx_ref[...] = idx


def _out_router(x2d, o_att, o_ssm, w_out_b, norm_w, w_router_b, b_router_p):
    T, D = x2d.shape
    att_w = o_att.shape[1]
    ssm_w = o_ssm.shape[1]
    bm = _tile(T, 512)
    row = lambda i: (i, 0)
    const = lambda i: (0, 0)
    return pl.pallas_call(
        functools.partial(_out_router_kernel, att_w=att_w),
        grid=(T // bm,),
        in_specs=[
            pl.BlockSpec((bm, D), row),
            pl.BlockSpec((bm, att_w), row),
            pl.BlockSpec((bm, ssm_w), row),
            pl.BlockSpec((att_w + ssm_w, D), const),
            pl.BlockSpec((1, D), const),
            pl.BlockSpec((D, LANES), const),
            pl.BlockSpec((1, LANES), const),
        ],
        out_specs=[
            pl.BlockSpec((bm, D), row),
            pl.BlockSpec((bm, D // 2), row),
            pl.BlockSpec((bm, LANES), row),
            pl.BlockSpec((bm, LANES), row),
        ],
        out_shape=[
            jax.ShapeDtypeStruct((T, D), F32),
            jax.ShapeDtypeStruct((T, D // 2), jnp.uint32),
            jax.ShapeDtypeStruct((T, LANES), F32),
            jax.ShapeDtypeStruct((T, LANES), jnp.int32),
        ],
        compiler_params=_cparams(("parallel",)),
        name="out_proj_router",
    )(x2d, o_att, o_ssm, w_out_b, norm_w.reshape(1, D), w_router_b, b_router_p)


def _moe_kernel(be_ref, br_ref, nu_ref, x_ref, wg_ref, wu_ref, bg_ref, bu_ref, wd_ref, bd_ref, y_ref,
                *scratch, sub):
    i = pl.program_id(0)
    f = pl.program_id(1)
    rows = br_ref[i]
    bm, half = x_ref.shape

    def unpack(xp):
        lo = lax.bitcast_convert_type(lax.shift_left(xp, jnp.uint32(16)), F32)
        hi = lax.bitcast_convert_type(xp & jnp.uint32(0xFFFF0000), F32)
        return jnp.concatenate([lo.astype(BF16), hi.astype(BF16)], axis=1)

    if scratch:
        xb_ref, = scratch

        @pl.when((f == 0) & (rows > 0))
        def _():
            xb_ref[...] = unpack(x_ref[...])

    for r in range(bm // sub):
        rs = slice(r * sub, (r + 1) * sub)

        @pl.when(f == 0)
        def _():
            live = jnp.where(rows > r * sub, 1.0, 0.0)
            y_ref[rs, :] = jnp.broadcast_to(bd_ref[0] * live, (sub, y_ref.shape[1]))

        @pl.when(rows > r * sub)
        def _():
            x = xb_ref[rs, :] if scratch else unpack(x_ref[rs, :])
            gate = jnp.minimum(_dot(x, wg_ref[0].astype(BF16)) + bg_ref[0], SWIGLU_LIMIT)
            up = jnp.clip(_dot(x, wu_ref[0].astype(BF16)) + bu_ref[0], -SWIGLU_LIMIT, SWIGLU_LIMIT)
            act = (gate * jax.nn.sigmoid(SWIGLU_ALPHA * gate) * (up + 1.0)).astype(BF16)
            y_ref[rs, :] += _dot(act, wd_ref[0].astype(BF16))


def _moe_ffn_blocks(x_sorted, block_e, block_rows, n_used, w_gate_up_l, b_gate_up_l, w_down_l, b_down_l, bm,
                    tf_pref=512, n_sub=2, stage_x=False):
    R, half = x_sorted.shape
    D = 2 * half
    E, _, two_f = w_gate_up_l.shape
    d_ff = two_f // 2
    tf = _tile(d_ff, tf_pref)
    n_f = d_ff // tf
    n_blocks = R // bm
    sub = bm // n_sub
    bgu = b_gate_up_l.reshape(E, 1, two_f)
    bd = b_down_l.reshape(E, 1, D)

    def fe(i, f, br):
        return jnp.where(br[i] > 0, f, n_f - 1)

    def xi(i, br, nu):
        return jnp.where(br[i] > 0, i, jnp.maximum(nu[0] - 1, 0))

    grid_spec = pltpu.PrefetchScalarGridSpec(
        num_scalar_prefetch=3,
        grid=(n_blocks, n_f),
        in_specs=[
            pl.BlockSpec((bm, half), lambda i, f, be, br, nu: (xi(i, br, nu), 0)),
            pl.BlockSpec((1, D, tf), lambda i, f, be, br, nu: (be[i], 0, fe(i, f, br))),
            pl.BlockSpec((1, D, tf), lambda i, f, be, br, nu: (be[i], 0, n_f + fe(i, f, br))),
            pl.BlockSpec((1, 1, tf), lambda i, f, be, br, nu: (be[i], 0, fe(i, f, br))),
            pl.BlockSpec((1, 1, tf), lambda i, f, be, br, nu: (be[i], 0, n_f + fe(i, f, br))),
            pl.BlockSpec((1, tf, D), lambda i, f, be, br, nu: (be[i], fe(i, f, br), 0)),
            pl.BlockSpec((1, 1, D), lambda i, f, be, br, nu: (be[i], 0, 0)),
        ],
        out_specs=pl.BlockSpec((bm, D), lambda i, f, be, br, nu: (i, 0)),
        scratch_shapes=[pltpu.VMEM((bm, D), BF16)] if stage_x else [],
    )
    return pl.pallas_call(
        functools.partial(_moe_kernel, sub=sub),
        grid_spec=grid_spec,
        out_shape=jax.ShapeDtypeStruct((R, D), F32),
        compiler_params=_cparams(("parallel", "arbitrary")),
        name="moe_ffn",
    )(block_e, block_rows, n_used, x_sorted, w_gate_up_l, w_gate_up_l, bgu, bgu, w_down_l, bd)


def _moe_route(top_e, n_experts, bm):
    T, K = top_e.shape
    M = T * K
    n_blocks = (M + n_experts * (bm - 1)) // bm
    flat_e = top_e.reshape(-1)
    order = jnp.argsort(flat_e)
    rank = jnp.argsort(order)
    counts = jnp.bincount(flat_e, length=n_experts)
    padded = ((counts + bm - 1) // bm) * bm
    start = jnp.cumsum(counts) - counts
    pend = jnp.cumsum(padded)
    pstart = pend - padded
    pad_off = pstart - start
    slot_of = (rank + pad_off[flat_e]).astype(jnp.int32)
    slots = jnp.arange(n_blocks * bm)
    slot_e = jnp.minimum(jnp.sum(slots[:, None] >= pend[None, :], axis=1), n_experts - 1)
    real = (slots - pstart[slot_e]) < counts[slot_e]
    src = jnp.clip(slots - pad_off[slot_e], 0, M - 1)
    slot_tok = jnp.where(real, order[src] // K, slots % T).astype(jnp.int32)
    n_used = pend[-1] // bm
    blk = jnp.arange(n_blocks)
    block_e = slot_e[::bm]
    block_rows = jnp.clip(counts[block_e] - (blk * bm - pstart[block_e]), 0, bm)
    block_rows = jnp.where(blk < n_used, block_rows, 0).astype(jnp.int32)
    last_e = block_e[jnp.maximum(n_used - 1, 0)]
    block_e = jnp.where(blk < n_used, block_e, last_e).astype(jnp.int32)
    return slot_tok, block_e, block_rows, n_used.astype(jnp.int32).reshape(1), slot_of.reshape(T, K)


def _combine_kernel(x_ref, *refs, final):
    yg_refs = refs[:TOP_K]
    gate_ref, nw_ref, out_ref = refs[TOP_K:]
    gate = gate_ref[...]
    y = x_ref[...]
    for k in range(TOP_K):
        y = y + gate[:, k:k + 1] * yg_refs[k][0]
    if final:
        ms = jnp.mean(y * y, axis=-1, keepdims=True)
        y = (y * lax.rsqrt(ms + EPS)) * nw_ref[...]
    out_ref[...] = y


def _combine(x1, yg, row0, gates, norm_w, final):
    T, D = x1.shape
    bm = _tile(math.gcd(T, row0) if row0 else T, 256)
    off = row0 // bm

    def yg_spec(k):
        return pl.BlockSpec((1, bm, D), lambda i: (k, off + i, 0))

    return pl.pallas_call(
        functools.partial(_combine_kernel, final=final),
        grid=(T // bm,),
        in_specs=[pl.BlockSpec((bm, D), lambda i: (i, 0))] + [yg_spec(k) for k in range(TOP_K)] + [
            pl.BlockSpec((bm, LANES), lambda i: (i, 0)),
            pl.BlockSpec((1, D), lambda i: (0, 0)),
        ],
        out_specs=pl.BlockSpec((bm, D), lambda i: (i, 0)),
        out_shape=jax.ShapeDtypeStruct((T, D), F32),
        compiler_params=_cparams(("parallel",)),
        name="moe_combine_norm",
    )(x1, *([yg] * TOP_K), gates, norm_w.reshape(1, D))


def kernel(x_prompt, x_sample, cache_k, cache_v, page_table, state_ssm, state_conv, meta_tokens,
           norm_mix_w, w_in, lambda_q1, lambda_k1, lambda_q2, lambda_k2, subln_w, conv_w, conv_b,
           dt_bias, a_log, d_skip, ssm_norm_w, w_out, norm_ffn_w, w_router, b_router,
           w_gate_up, b_gate_up, w_down, b_down, norm_final_w):
    depth = w_in.shape[0]
    Bp, Lp, D = x_prompt.shape
    Bs, Ss, _ = x_sample.shape
    n_heads_att = cache_v.shape[3]
    att_w = n_heads_att * ATT_V_DIM
    qk_w = n_heads_att * 2 * ATT_HEAD_DIM
    n_heads_ssm = state_ssm.shape[2]
    ssm_w = n_heads_ssm * SSM_HEAD_DIM
    conv_dim = state_conv.shape[-1]
    n_main = 2 * qk_w + att_w + ssm_w + conv_dim
    n_experts = w_router.shape[-1]
    n_pool, page = cache_k.shape[1], cache_k.shape[2]
    Tp, Ts = Bp * Lp, Bs * Ss
    Tsm = Ts + N_META
    chunk = 128

    xp = x_prompt.reshape(Tp, D)
    xsm = jnp.concatenate([x_sample.reshape(Ts, D), meta_tokens.astype(F32)], axis=0)

    outs = {k: [] for k in ("kp", "vp", "ks", "vs", "ssmp", "convp", "ssms", "convs")}
    for layer in range(depth):
        lam_init = _lambda_init(layer)
        lam_vec = jnp.stack([lambda_q1[layer], lambda_k1[layer], lambda_q2[layer], lambda_k2[layer]])
        w_main = w_in[layer][:, :n_main].astype(BF16)
        w_dt = jnp.pad(w_in[layer][:, n_main:], ((0, 0), (0, LANES - n_heads_ssm))).astype(BF16)
        pad_h = (0, LANES - n_heads_ssm)
        ssd_p = {
            "conv_w": conv_w[layer], "conv_b": conv_b[layer].reshape(1, conv_dim),
            "dt_bias": jnp.pad(dt_bias[layer], pad_h).reshape(1, LANES),
            "dt_bias_t": dt_bias[layer].reshape(n_heads_ssm, 1),
            "a_log": jnp.pad(a_log[layer], pad_h).reshape(1, LANES),
            "a_log_t": a_log[layer].reshape(n_heads_ssm, 1),
            "d_skip_x": jnp.repeat(d_skip[layer], SSM_HEAD_DIM).reshape(1, ssm_w),
            "norm_w": ssm_norm_w[layer].reshape(1, ssm_w),
            "expand": (jnp.arange(LANES)[:, None] == (jnp.arange(ssm_w) // SSM_HEAD_DIM)[None, :]).astype(BF16),
        }

        proj_p, dt_p = _rms_proj(xp, norm_mix_w[layer], w_main, w_dt)
        proj_sm, dt_sm = _rms_proj(xsm, norm_mix_w[layer], w_main, w_dt)
        proj_p3 = proj_p.reshape(Bp, Lp, n_main)
        proj_s3 = proj_sm[:Ts].reshape(Bs, Ss, n_main)
        proj_m = proj_sm[Ts:]

        def pad_rows(a, rows):
            return jnp.pad(a, ((0, 0), (0, rows - a.shape[1]), (0, 0)))

        proj_m_pad = pad_rows(proj_m[None], chunk)
        oa_m = _flash_attn(proj_m_pad, None, 0, lam_vec, subln_w[layer], lam_init, n_heads_att)
        ck = jnp.transpose(cache_k[layer], (0, 2, 3, 4, 1)).reshape(n_pool, qk_w, page)
        cv = cache_v[layer].reshape(n_pool, page * n_heads_att, ATT_V_DIM)
        oa_p, oa_s = _attn_fused(proj_p3, proj_m_pad[0], N_META, proj_s3, ck, cv, page_table, lam_vec,
                                 subln_w[layer], lam_init, n_heads_att)

        zero_conv = jnp.zeros((1, CONV_PAD, conv_dim), F32)
        zero_state = jnp.zeros((1, SSM_GROUPS, SSM_STATE, ssm_w // SSM_GROUPS), F32)
        ys_m, conv_m, st_m = _ssd(proj_m_pad, pad_rows(dt_sm[Ts:][None], chunk),
                                  zero_conv, zero_state, True, N_META, ssd_p, n_heads_ssm)
        ys_p, conv_p, st_p = _ssd(proj_p3, dt_p.reshape(Bp, Lp, LANES), conv_m, st_m, True, chunk,
                                  ssd_p, n_heads_ssm)
        conv0_s = jnp.pad(state_conv[layer].astype(F32), ((0, 0), (CONV_PAD - (CONV_WIDTH - 1), 0), (0, 0)))
        ys_s, conv_s, st_s = _ssd(pad_rows(proj_s3, chunk), pad_rows(dt_sm[:Ts].reshape(Bs, Ss, LANES), chunk),
                                  conv0_s, _state_to_t(state_ssm[layer].astype(F32), n_heads_ssm),
                                  False, Ss, ssd_p, n_heads_ssm)

        w_out_b = w_out[layer].astype(BF16)
        w_router_b = jnp.pad(w_router[layer], ((0, 0), (0, LANES - n_experts))).astype(BF16)
        b_router_p = jnp.pad(b_router[layer].astype(F32), (0, LANES - n_experts),
                             constant_values=-1e30).reshape(1, LANES)
        oa_sm = jnp.concatenate([oa_s.reshape(Ts, att_w).astype(BF16), oa_m[0, :N_META]], axis=0)
        os_sm = jnp.concatenate([ys_s[:, :Ss].reshape(Ts, ssm_w), ys_m[0, :N_META]], axis=0)
        x1_p, h2_p, gate_p, idx_p = _out_router(xp, oa_p.reshape(Tp, att_w), ys_p.reshape(Tp, ssm_w),
                                                w_out_b, norm_ffn_w[layer], w_router_b, b_router_p)
        x1_sm, h2_sm, gate_sm, idx_sm = _out_router(xsm, oa_sm, os_sm, w_out_b, norm_ffn_w[layer],
                                                    w_router_b, b_router_p)

        h2 = jnp.concatenate([h2_p, h2_sm], axis=0)
        top_e = jnp.concatenate([idx_p[:, :TOP_K], idx_sm[:, :TOP_K]], axis=0)
        T_all = Tp + Tsm
        bm_moe = 1024 if T_all * TOP_K >= 1024 * n_experts else 256
        slot_tok, block_e, block_rows, n_used, slot_of = _moe_route(top_e, n_experts, bm_moe)
        x_sorted = h2[slot_tok]
        yb = _moe_ffn_blocks(x_sorted, block_e, block_rows, n_used, w_gate_up[layer], b_gate_up[layer],
                             w_down[layer], b_down[layer], bm_moe)
        yg = yb[slot_of.T.reshape(-1)].reshape(TOP_K, T_all, D)
        final = layer == depth - 1
        xp = _combine(x1_p, yg, 0, gate_p, norm_final_w, final)
        xsm = _combine(x1_sm, yg, Tp, gate_sm, norm_final_w, final)

        k_m = jnp.broadcast_to(proj_m[None, :, qk_w:2 * qk_w], (Bp, N_META, qk_w))
        v_m = jnp.broadcast_to(proj_m[None, :, 2 * qk_w:2 * qk_w + att_w], (Bp, N_META, att_w))
        k_p = jnp.concatenate([k_m, proj_p3[:, :, qk_w:2 * qk_w]], axis=1)
        v_p = jnp.concatenate([v_m, proj_p3[:, :, 2 * qk_w:2 * qk_w + att_w]], axis=1)
        outs["kp"].append(k_p.reshape(Bp, N_META + Lp, n_heads_att, 2, ATT_HEAD_DIM))
        outs["vp"].append(v_p.reshape(Bp, N_META + Lp, n_heads_att, ATT_V_DIM))
        outs["ks"].append(proj_s3[:, :, qk_w:2 * qk_w].reshape(Bs, Ss, n_heads_att, 2, ATT_HEAD_DIM))
        outs["vs"].append(proj_s3[:, :, 2 * qk_w:2 * qk_w + att_w].reshape(Bs, Ss, n_heads_att, ATT_V_DIM))
        outs["ssmp"].append(_state_from_t(st_p, n_heads_ssm))
        outs["convp"].append(conv_p[:, CONV_PAD - (CONV_WIDTH - 1):])
        outs["ssms"].append(_state_from_t(st_s, n_heads_ssm))
        outs["convs"].append(conv_s[:, CONV_PAD - (CONV_WIDTH - 1):])

    y_prompt = xp.reshape(Bp, Lp, D)
    y_sample = xsm[:Ts].reshape(Bs, Ss, D)
    return (y_prompt, y_sample, jnp.stack(outs["kp"]), jnp.stack(outs["vp"]), jnp.stack(outs["ks"]),
            jnp.stack(outs["vs"]), jnp.stack(outs["ssmp"]), jnp.stack(outs["convp"]),
            jnp.stack(outs["ssms"]), jnp.stack(outs["convs"]))
```
